```python
import jax, jax.numpy as jnp
from jax import lax
import numpy as np

D_MODEL = 1024
BATCH = 4
SEQ = 8192
DEPTH = 2

CHUNK = 64
Q_BLOCK = 128
D_FF = 2816
BRANCH_WIDTH = 512
N_BRANCH = 3
LRU_WIDTH = BRANCH_WIDTH
LRU_BLOCKS = 8
LRU_BLOCK = LRU_WIDTH // LRU_BLOCKS
CONV_WIDTH = 4
LRU_C = 8.0
GLA_HEADS = 4
GLA_DV = BRANCH_WIDTH // GLA_HEADS
GLA_DK = GLA_DV // 2
GLA_LOWRANK = 16
GLA_TAU = 16.0
FOX_HEADS = 8
FOX_DH = BRANCH_WIDTH // FOX_HEADS
PLE_DIM = 256
LN_EPS = 1e-5
RMS_EPS = 1e-6
DEEPNORM_ALPHA = (2 * DEPTH) ** 0.25
DEEPNORM_BETA = (8 * DEPTH) ** -0.25

SPLIT_SIZES = (
    LRU_WIDTH,
    LRU_WIDTH,
    GLA_HEADS * GLA_DK,
    GLA_HEADS * GLA_DK,
    GLA_HEADS * GLA_DV,
    GLA_LOWRANK,
    GLA_HEADS * GLA_DV,
    FOX_HEADS * FOX_DH,
    FOX_HEADS * FOX_DH,
    FOX_HEADS * FOX_DH,
    FOX_HEADS,
    N_BRANCH * D_MODEL,
)
SPLIT_POINTS = tuple(int(v) for v in np.cumsum(SPLIT_SIZES)[:-1])
D_IN = int(sum(SPLIT_SIZES))

kernel_name = 'hybrid_rglru_gla_fox_macaron_deepnorm'


def layer_norm(x, g, b):
    xf = x.astype(jnp.float32)
    mu = jnp.mean(xf, axis=-1, keepdims=True)
    var = jnp.mean(jnp.square(xf - mu), axis=-1, keepdims=True)
    y = (xf - mu) * lax.rsqrt(var + LN_EPS) * g.astype(jnp.float32) + b.astype(jnp.float32)
    return y.astype(x.dtype)


def swiglu(x, w_up, w_down):
    gate, up = jnp.split(x @ w_up, 2, axis=-1)
    return (jax.nn.silu(gate) * up) @ w_down


def linear_scan(a, b):
    def combine(left, right):
        return (left[0] * right[0], right[0] * left[1] + right[1])
    return lax.associative_scan(combine, (a, b), axis=1)[1]


def causal_depthwise_conv(x, w, b):
    y = lax.conv_general_dilated(
        x, w[:, None, :], window_strides=(1,), padding=[(CONV_WIDTH - 1, 0)],
        dimension_numbers=('NWC', 'WIO', 'NWC'), feature_group_count=x.shape[-1])
    return y + b


def rg_lru(x, wa, ba, wx, bx, lam):
    bsz, seq, width = x.shape
    f32 = jnp.float32
    xf = x.astype(f32)
    xb = xf.reshape(bsz, seq, LRU_BLOCKS, LRU_BLOCK)
    r = jax.nn.sigmoid(jnp.einsum('bsnc,ncd->bsnd', xb, wa.astype(f32)).reshape(bsz, seq, width) + ba.astype(f32))
    i = jax.nn.sigmoid(jnp.einsum('bsnc,ncd->bsnd', xb, wx.astype(f32)).reshape(bsz, seq, width) + bx.astype(f32))
    log_a = -LRU_C * r * jax.nn.softplus(-lam.astype(f32))
    a = jnp.exp(log_a)
    mult = jnp.sqrt(-jnp.expm1(2.0 * log_a))
    h = linear_scan(a, mult * (i * xf))
    return h.astype(x.dtype)


def gla(q, k, v, g_low, w_g2, b_g, norm_g, out_gate):
    bsz, seq, _ = q.shape
    nc = seq // CHUNK
    f32 = jnp.float32
    qc = q.astype(f32).reshape(bsz, nc, CHUNK, GLA_HEADS, GLA_DK) * (GLA_DK ** -0.5)
    kc = k.astype(f32).reshape(bsz, nc, CHUNK, GLA_HEADS, GLA_DK)
    vc = v.astype(f32).reshape(bsz, nc, CHUNK, GLA_HEADS, GLA_DV)
    log_alpha = jax.nn.log_sigmoid((g_low @ w_g2 + b_g).astype(f32)) / GLA_TAU
    log_alpha = log_alpha.reshape(bsz, nc, CHUNK, GLA_HEADS, GLA_DK)
    g_cum = jnp.cumsum(log_alpha, axis=2)
    g_tot = g_cum[:, :, -1]
    k_dec = kc * jnp.exp(g_tot[:, :, None] - g_cum)
    delta = jnp.einsum('bnchk,bnchv->bnhkv', k_dec, vc)
    state = linear_scan(jnp.exp(g_tot)[..., None], delta)
    o = jnp.einsum('bnchk,bnhkv->bnchv', qc, state)
    o = o * lax.rsqrt(jnp.mean(jnp.square(o), axis=-1, keepdims=True) + RMS_EPS)
    o = o.reshape(bsz, seq, GLA_HEADS * GLA_DV) * norm_g.astype(f32)
    return o.astype(q.dtype) * jax.nn.silu(out_gate)


def forgetting_attention(q, k, v, f_logit, b_f):
    bsz, seq, _ = q.shape
    nb = seq // Q_BLOCK
    f32 = jnp.float32
    scale = FOX_DH ** -0.5
    qh = q.reshape(bsz, seq, FOX_HEADS, FOX_DH)
    kh = k.reshape(bsz, seq, FOX_HEADS, FOX_DH).transpose(0, 2, 1, 3)
    vh = v.reshape(bsz, seq, FOX_HEADS, FOX_DH).transpose(0, 2, 1, 3)
    log_f = jax.nn.log_sigmoid((f_logit + b_f).astype(f32))
    f_cum = jnp.cumsum(log_f, axis=1).transpose(0, 2, 1)
    q_blk = qh.reshape(bsz, nb, Q_BLOCK, FOX_HEADS, FOX_DH).transpose(1, 0, 3, 2, 4)
    f_blk = f_cum.reshape(bsz, FOX_HEADS, nb, Q_BLOCK).transpose(2, 0, 1, 3)
    kpos = jnp.arange(seq)

    def attend(args):
        qb, fb, n = args
        logits = jnp.einsum('bhqd,bhkd->bhqk', qb, kh).astype(f32) * scale
        logits = logits + (fb[..., None] - f_cum[:, :, None, :])
        qpos = n * Q_BLOCK + jnp.arange(Q_BLOCK)
        mask = kpos[None, :] <= qpos[:, None]
        probs = jax.nn.softmax(jnp.where(mask, logits, -jnp.inf), axis=-1)
        return jnp.einsum('bhqk,bhkd->bhqd', probs.astype(vh.dtype), vh)

    out = lax.map(attend, (q_blk, f_blk, jnp.arange(nb)))
    return out.transpose(1, 0, 3, 2, 4).reshape(bsz, seq, FOX_HEADS * FOX_DH)


def hybrid_mixer(x, w_in, conv_w, conv_b, lru_wa, lru_ba, lru_wx, lru_bx, lru_lambda,
                 gla_w_g2, gla_b_g, gla_norm_g, fox_b_f, w_branch, w_out):
    bsz, seq, _ = x.shape
    (a_x, a_y, b_q, b_k, b_v, b_low, b_r,
     c_q, c_k, c_v, c_f, gate_logits) = jnp.split(x @ w_in, SPLIT_POINTS, axis=-1)
    y_a = jax.nn.gelu(a_y) * rg_lru(causal_depthwise_conv(a_x, conv_w, conv_b),
                                    lru_wa, lru_ba, lru_wx, lru_bx, lru_lambda)
    y_b = gla(b_q, b_k, b_v, b_low, gla_w_g2, gla_b_g, gla_norm_g, b_r)
    y_c = forgetting_attention(c_q, c_k, c_v, c_f, fox_b_f)
    gates = jax.nn.sigmoid(gate_logits).reshape(bsz, seq, N_BRANCH, D_MODEL)
    merged = (gates[:, :, 0] * (y_a @ w_branch[0])
              + gates[:, :, 1] * (y_b @ w_branch[1])
              + gates[:, :, 2] * (y_c @ w_branch[2]))
    return merged @ w_out


def setup_inputs(seed: int = 0) -> dict:
    key = jax.random.key(seed)
    ks = jax.random.split(key, 40)
    f32 = jnp.float32

    def nrm(k, shape, scale):
        return jax.random.normal(k, shape, f32) * scale

    def gain(k, shape):
        return 1.0 + 0.02 * jax.random.normal(k, shape, f32)

    u = jax.random.uniform(ks[12], (DEPTH, LRU_WIDTH), f32, minval=0.9, maxval=0.999)
    a0 = u ** (1.0 / LRU_C)
    lru_lambda = jnp.log(a0) - jnp.log1p(-a0)

    return {
        'x': nrm(ks[0], (BATCH, SEQ, D_MODEL), 1.0),
        'p': nrm(ks[1], (DEPTH, BATCH, SEQ, PLE_DIM), 1.0),
        'ffn1_w_up': nrm(ks[2], (DEPTH, D_MODEL, 2 * D_FF), D_MODEL ** -0.5),
        'ffn1_w_down': nrm(ks[3], (DEPTH, D_FF, D_MODEL), D_FF ** -0.5 * DEEPNORM_BETA),
        'ln1_g': gain(ks[4], (DEPTH, D_MODEL)),
        'ln1_b': nrm(ks[5], (DEPTH, D_MODEL), 0.02),
        'w_in': nrm(ks[6], (DEPTH, D_MODEL, D_IN), D_MODEL ** -0.5),
        'conv_w': nrm(ks[7], (DEPTH, CONV_WIDTH, LRU_WIDTH), CONV_WIDTH ** -0.5),
        'conv_b': nrm(ks[8], (DEPTH, LRU_WIDTH), 0.02),
        'lru_wa': nrm(ks[9], (DEPTH, LRU_BLOCKS, LRU_BLOCK, LRU_BLOCK), LRU_BLOCK ** -0.5),
        'lru_ba': nrm(ks[10], (DEPTH, LRU_WIDTH), 0.02),
        'lru_wx': nrm(ks[11], (DEPTH, LRU_BLOCKS, LRU_BLOCK, LRU_BLOCK), LRU_BLOCK ** -0.5),
        'lru_bx': nrm(ks[13], (DEPTH, LRU_WIDTH), 0.02),
        'lru_lambda': lru_lambda,
        'gla_w_g2': nrm(ks[14], (DEPTH, GLA_LOWRANK, GLA_HEADS * GLA_DK), GLA_LOWRANK ** -0.5),
        'gla_b_g': nrm(ks[15], (DEPTH, GLA_HEADS * GLA_DK), 0.02),
        'gla_norm_g': gain(ks[16], (DEPTH, GLA_HEADS * GLA_DV)),
        'fox_b_f': jax.random.uniform(ks[17], (DEPTH, FOX_HEADS), f32, minval=1.0, maxval=4.0),
        'w_branch': nrm(ks[18], (DEPTH, N_BRANCH, BRANCH_WIDTH, D_MODEL), BRANCH_WIDTH ** -0.5),
        'w_out': nrm(ks[19], (DEPTH, D_MODEL, D_MODEL), D_MODEL ** -0.5 * DEEPNORM_BETA),
        'ln2_g': gain(ks[20], (DEPTH, D_MODEL)),
        'ln2_b': nrm(ks[21], (DEPTH, D_MODEL), 0.02),
        'ffn2_w_up': nrm(ks[22], (DEPTH, D_MODEL, 2 * D_FF), D_MODEL ** -0.5),
        'ffn2_w_down': nrm(ks[23], (DEPTH, D_FF, D_MODEL), D_FF ** -0.5 * DEEPNORM_BETA),
        'ln3_g': gain(ks[24], (DEPTH, D_MODEL)),
        'ln3_b': nrm(ks[25], (DEPTH, D_MODEL), 0.02),
        'ple_w_proj': nrm(ks[26], (DEPTH, PLE_DIM, D_MODEL), PLE_DIM ** -0.5 * DEEPNORM_BETA),
        'ple_w_gate': nrm(ks[27], (DEPTH, D_MODEL, D_MODEL), D_MODEL ** -0.5),
        'ple_b_gate': nrm(ks[28], (DEPTH, D_MODEL), 0.02),
        'ln4_g': gain(ks[29], (DEPTH, D_MODEL)),
        'ln4_b': nrm(ks[30], (DEPTH, D_MODEL), 0.02),
    }


def reference(x, p, ffn1_w_up, ffn1_w_down, ln1_g, ln1_b, w_in, conv_w, conv_b,
              lru_wa, lru_ba, lru_wx, lru_bx, lru_lambda, gla_w_g2, gla_b_g, gla_norm_g,
              fox_b_f, w_branch, w_out, ln2_g, ln2_b, ffn2_w_up, ffn2_w_down, ln3_g, ln3_b,
              ple_w_proj, ple_w_gate, ple_b_gate, ln4_g, ln4_b):
    for i in range(DEPTH):
        x = layer_norm(DEEPNORM_ALPHA * x + 0.5 * swiglu(x, ffn1_w_up[i], ffn1_w_down[i]),
                       ln1_g[i], ln1_b[i])
        mix = hybrid_mixer(x, w_in[i], conv_w[i], conv_b[i], lru_wa[i], lru_ba[i], lru_wx[i],
                           lru_bx[i], lru_lambda[i], gla_w_g2[i], gla_b_g[i], gla_norm_g[i],
                           fox_b_f[i], w_branch[i], w_out[i])
        x = layer_norm(DEEPNORM_ALPHA * x + mix, ln2_g[i], ln2_b[i])
        x = layer_norm(DEEPNORM_ALPHA * x + 0.5 * swiglu(x, ffn2_w_up[i], ffn2_w_down[i]),
                       ln3_g[i], ln3_b[i])
        pe = p[i] @ ple_w_proj[i]
        x = layer_norm(DEEPNORM_ALPHA * x + jax.nn.sigmoid(x @ ple_w_gate[i] + ple_b_gate[i]) * pe,
                       ln4_g[i], ln4_b[i])
    return x
```

```python
import functools

import jax
import jax.numpy as jnp
from jax import lax
from jax.experimental import pallas as pl
from jax.experimental.pallas import tpu as pltpu

F32 = jnp.float32
BF16 = jnp.bfloat16

DEPTH = 2
CHUNK = 64
LRU_BLOCKS = 8
CONV_WIDTH = 4
LRU_C = 8.0
GLA_HEADS = 4
GLA_LOWRANK = 16
GLA_TAU = 16.0
FOX_HEADS = 8
N_BRANCH = 3
LN_EPS = 1e-5
RMS_EPS = 1e-6
DEEPNORM_ALPHA = (2 * DEPTH) ** 0.25

LANES = 128
VMEM_LIMIT = 56 * 1024 * 1024

FFN_TM = 512
PROJ_TM = 512
LRU_T = 512
GLA_T = 512
ATT_TQ = 512
ATT_TK = 512
MERGE_TM = 512
PLE_TM = 512


def _const_spec(shape):
    nd = len(shape)
    return pl.BlockSpec(shape, lambda *_: (0,) * nd, pipeline_mode=pl.Buffered(1))


def _params(*sem):
    return pltpu.CompilerParams(dimension_semantics=sem, vmem_limit_bytes=VMEM_LIMIT)


def _layer_norm(z, g, b):
    mu = jnp.mean(z, axis=-1, keepdims=True)
    d = z - mu
    var = jnp.mean(d * d, axis=-1, keepdims=True)
    return d * lax.rsqrt(var + LN_EPS) * g + b


def _softplus(z):
    return jnp.maximum(z, 0.0) + jnp.log1p(jnp.exp(-jnp.abs(z)))


def _log_sigmoid(z):
    return jnp.minimum(z, 0.0) - jnp.log1p(jnp.exp(-jnp.abs(z)))


def _silu(z):
    return z * jax.nn.sigmoid(z)


def _cumsum_rows(x, period):
    rows = x.shape[0]
    pos = lax.broadcasted_iota(jnp.int32, x.shape, 0) % period
    s = 1
    while s < period:
        x = x + jnp.where(pos >= s, pltpu.roll(x, s, axis=0), 0.0)
        s *= 2
    del rows
    return x


def _ffn_kernel(x_ref, wg_ref, wu_ref, wd_ref, g_ref, b_ref, o_ref):
    x = x_ref[...]
    xb = x.astype(BF16)
    gate = jnp.dot(xb, wg_ref[...], preferred_element_type=F32)
    up = jnp.dot(xb, wu_ref[...], preferred_element_type=F32)
    act = (_silu(gate) * up).astype(BF16)
    y = jnp.dot(act, wd_ref[...], preferred_element_type=F32)
    o_ref[...] = _layer_norm(DEEPNORM_ALPHA * x + 0.5 * y, g_ref[...], b_ref[...])


def _ffn(x2, w_gate, w_up, w_down, g, b):
    n, d = x2.shape
    dff = w_gate.shape[1]
    tm = min(FFN_TM, n)
    return pl.pallas_call(
        _ffn_kernel,
        grid=(n // tm,),
        in_specs=[pl.BlockSpec((tm, d), lambda i: (i, 0)),
                  _const_spec((d, dff)), _const_spec((d, dff)), _const_spec((dff, d)),
                  _const_spec((1, d)), _const_spec((1, d))],
        out_specs=pl.BlockSpec((tm, d), lambda i: (i, 0)),
        out_shape=jax.ShapeDtypeStruct((n, d), F32),
        compiler_params=_params("parallel"),
        name="ffn_ln",
    )(x2, w_gate, w_up, w_down, g, b)


def _proj_kernel(x_ref, w_ref, bf_ref, lru_ref, gla_ref, fq_ref, fk_ref, fv_ref,
                 small_ref, fcum_ref, carry_ref, *, widths):
    @pl.when(pl.program_id(1) == 0)
    def _():
        carry_ref[...] = jnp.zeros_like(carry_ref)

    xb = x_ref[0].astype(BF16)
    y = jnp.dot(xb, w_ref[...], preferred_element_type=F32)
    w_lru, w_gla, w_fox = widths
    o = 0
    lru_ref[0] = y[:, o:o + w_lru].astype(BF16); o += w_lru
    gla_ref[0] = y[:, o:o + w_gla].astype(BF16); o += w_gla
    fq_ref[0] = y[:, o:o + w_fox].astype(BF16); o += w_fox
    fk_ref[0] = y[:, o:o + w_fox].astype(BF16); o += w_fox
    fv_ref[0] = y[:, o:o + w_fox].astype(BF16); o += w_fox
    small = y[:, o:o + LANES]
    small_ref[0] = small
    log_f = _log_sigmoid(small + bf_ref[...])
    f_cum = _cumsum_rows(log_f, log_f.shape[0]) + carry_ref[0:1, :]
    fcum_ref[0] = f_cum
    carry_ref[0:1, :] = f_cum[-1:, :]


def _proj(x, w_proj, bf_pad, widths):
    bsz, seq, d = x.shape
    tm = min(PROJ_TM, seq)
    w_lru, w_gla, w_fox = widths
    n_out = w_proj.shape[1]
    tok = lambda w: pl.BlockSpec((1, tm, w), lambda b, i: (b, i, 0))
    return pl.pallas_call(
        functools.partial(_proj_kernel, widths=widths),
        grid=(bsz, seq // tm),
        in_specs=[tok(d), _const_spec((d, n_out)), _const_spec((1, LANES))],
        out_specs=[tok(w_lru), tok(w_gla), tok(w_fox), tok(w_fox), tok(w_fox),
                   tok(LANES), tok(LANES)],
        out_shape=[jax.ShapeDtypeStruct((bsz, seq, w_lru), BF16),
                   jax.ShapeDtypeStruct((bsz, seq, w_gla), BF16),
                   jax.ShapeDtypeStruct((bsz, seq, w_fox), BF16),
                   jax.ShapeDtypeStruct((bsz, seq, w_fox), BF16),
                   jax.ShapeDtypeStruct((bsz, seq, w_fox), BF16),
                   jax.ShapeDtypeStruct((bsz, seq, LANES), F32),
                   jax.ShapeDtypeStruct((bsz, seq, LANES), F32)],
        scratch_shapes=[pltpu.VMEM((8, LANES), F32)],
        compiler_params=_params("parallel", "arbitrary"),
        name="mixer_in_proj",
    )(x, w_proj, bf_pad)


def _lru_kernel(in_ref, cw_ref, cb_ref, wg_ref, ba_ref, bx_ref, lam_ref, o_ref,
                halo_ref, h_ref, *, width):
    @pl.when(pl.program_id(1) == 0)
    def _():
        halo_ref[...] = jnp.zeros_like(halo_ref)
        h_ref[...] = jnp.zeros_like(h_ref)

    tile = in_ref[0]
    ax = tile[:, :width].astype(F32)
    ay = tile[:, width:].astype(F32)
    rows = ax.shape[0]

    ext = jnp.concatenate([halo_ref[...], ax], axis=0)
    u = cb_ref[...] + cw_ref[CONV_WIDTH - 1:CONV_WIDTH, :] * ax
    for k in range(CONV_WIDTH - 1):
        shift = CONV_WIDTH - 1 - k
        u = u + cw_ref[k:k + 1, :] * ext[8 - shift:8 - shift + rows, :]
    halo_ref[...] = ax[rows - 8:, :]

    gates = jnp.dot(u.astype(BF16), wg_ref[...], preferred_element_type=F32)
    r = jax.nn.sigmoid(gates[:, :width] + ba_ref[...])
    i = jax.nn.sigmoid(gates[:, width:] + bx_ref[...])
    log_a = (-LRU_C) * r * _softplus(-lam_ref[...])
    a = jnp.exp(log_a)
    mult = jnp.sqrt(-jnp.tanh(log_a) * (1.0 + a * a))
    bterm = mult * (i * u)

    pos = lax.broadcasted_iota(jnp.int32, a.shape, 0)
    s = 1
    while s < rows:
        keep = pos >= s
        a_sh = jnp.where(keep, pltpu.roll(a, s, axis=0), 1.0)
        b_sh = jnp.where(keep, pltpu.roll(bterm, s, axis=0), 0.0)
        bterm = bterm + a * b_sh
        a = a * a_sh
        s *= 2
    h = bterm + a * h_ref[0:1, :]
    h_ref[0:1, :] = h[rows - 1:, :]
    o_ref[0] = (jax.nn.gelu(ay) * h).astype(BF16)


def _lru(lru_in, conv_w, conv_b, w_gates, ba, bx, lam):
    bsz, seq, w2 = lru_in.shape
    width = w2 // 2
    t = min(LRU_T, seq)
    return pl.pallas_call(
        functools.partial(_lru_kernel, width=width),
        grid=(bsz, seq // t),
        in_specs=[pl.BlockSpec((1, t, w2), lambda b, i: (b, i, 0)),
                  _const_spec((CONV_WIDTH, width)), _const_spec((1, width)),
                  _const_spec((width, w2)), _const_spec((1, width)),
                  _const_spec((1, width)), _const_spec((1, width))],
        out_specs=pl.BlockSpec((1, t, width), lambda b, i: (b, i, 0)),
        out_shape=jax.ShapeDtypeStruct((bsz, seq, width), BF16),
        scratch_shapes=[pltpu.VMEM((8, width), F32), pltpu.VMEM((8, width), F32)],
        compiler_params=_params("parallel", "arbitrary"),
        name="rg_lru",
    )(lru_in, conv_w, conv_b, w_gates, ba, bx, lam)


def _gla_kernel(in_ref, small_ref, wg2_ref, bg_ref, ng_ref, mask_ref, o_ref, st_ref,
                *, dk_all, dv_all):
    @pl.when(pl.program_id(1) == 0)
    def _():
        st_ref[...] = jnp.zeros_like(st_ref)

    rows = in_ref.shape[1]
    dv = dv_all // GLA_HEADS
    dk = dk_all // GLA_HEADS
    low = small_ref[0].astype(BF16)
    g = jnp.dot(low, wg2_ref[...], preferred_element_type=F32) + bg_ref[...]
    log_alpha = _log_sigmoid(g) * (1.0 / GLA_TAU)
    g_cum = _cumsum_rows(log_alpha, CHUNK)

    for c in range(rows // CHUNK):
        sl = slice(c * CHUNK, (c + 1) * CHUNK)
        blk = in_ref[0, sl, :]
        q = (blk[:, :dk_all].astype(F32) * (dk ** -0.5)).astype(BF16)
        k = blk[:, dk_all:2 * dk_all].astype(F32)
        v = blk[:, 2 * dk_all:2 * dk_all + dv_all]
        gate = blk[:, 2 * dk_all + dv_all:].astype(F32)
        gc = g_cum[sl, :]
        g_tot = gc[CHUNK - 1:, :]
        k_dec = (k * jnp.exp(g_tot - gc)).astype(BF16)
        delta = lax.dot_general(v, k_dec, (((0,), (0,)), ((), ())),
                                preferred_element_type=F32)
        st = st_ref[...] * jnp.exp(g_tot) + delta * mask_ref[...]
        st_ref[...] = st
        o = lax.dot_general(q, st.astype(BF16), (((1,), (1,)), ((), ())),
                            preferred_element_type=F32)
        outs = []
        for h in range(GLA_HEADS):
            oh = o[:, h * dv:(h + 1) * dv]
            ms = jnp.mean(oh * oh, axis=-1, keepdims=True)
            outs.append(oh * lax.rsqrt(ms + RMS_EPS))
        o = jnp.concatenate(outs, axis=-1) * ng_ref[...]
        o_ref[0, sl, :] = (o * _silu(gate)).astype(BF16)


def _gla(gla_in, small, w_g2_pad, b_g, norm_g, mask, dk_all, dv_all):
    bsz, seq, w = gla_in.shape
    t = min(GLA_T, seq)
    return pl.pallas_call(
        functools.partial(_gla_kernel, dk_all=dk_all, dv_all=dv_all),
        grid=(bsz, seq // t),
        in_specs=[pl.BlockSpec((1, t, w), lambda b, i: (b, i, 0)),
                  pl.BlockSpec((1, t, LANES), lambda b, i: (b, i, 0)),
                  _const_spec((LANES, dk_all)), _const_spec((1, dk_all)),
                  _const_spec((1, dv_all)), _const_spec((dv_all, dk_all))],
        out_specs=pl.BlockSpec((1, t, dv_all), lambda b, i: (b, i, 0)),
        out_shape=jax.ShapeDtypeStruct((bsz, seq, dv_all), BF16),
        scratch_shapes=[pltpu.VMEM((dv_all, dk_all), F32)],
        compiler_params=_params("parallel", "arbitrary"),
        name="gla",
    )(gla_in, small, w_g2_pad, b_g, norm_g, mask)


def _fox_kernel(q_ref, k_ref, v_ref, f_ref, o_ref, *, tk, dh):
    tq = q_ref.shape[1]
    i = pl.program_id(2)
    q = q_ref[0]
    lane = lax.broadcasted_iota(jnp.int32, q.shape, 1)
    zero = jnp.zeros_like(q)
    qs = (jnp.where(lane < dh, q, zero), jnp.where(lane >= dh, q, zero))

    def step(j, carry, masked):
        off = pl.multiple_of(j * tk, tk)
        kt = k_ref[0, pl.ds(off, tk), :]
        vt = v_ref[0, pl.ds(off, tk), :]
        new = []
        for hh in range(2):
            m, l, acc = carry[hh]
            s = lax.dot_general(qs[hh], kt, (((1,), (1,)), ((), ())),
                                preferred_element_type=F32)
            s = s - f_ref[0, 0, hh, pl.ds(j, 1), :]
            if masked:
                qpos = lax.broadcasted_iota(jnp.int32, s.shape, 0)
                kpos = lax.broadcasted_iota(jnp.int32, s.shape, 1)
                s = jnp.where(kpos <= qpos, s, -jnp.inf)
            m_new = jnp.maximum(m, jnp.max(s, axis=-1, keepdims=True))
            alpha = jnp.exp(m - m_new)
            p = jnp.exp(s - m_new)
            l = alpha * l + jnp.sum(p, axis=-1, keepdims=True)
            acc = alpha * acc + jnp.dot(p.astype(BF16), vt, preferred_element_type=F32)
            new.append((m_new, l, acc))
        return tuple(new)

    init = tuple((jnp.full((tq, 1), -jnp.inf, F32), jnp.zeros((tq, 1), F32),
                  jnp.zeros((tq, 2 * dh), F32)) for _ in range(2))
    carry = lax.fori_loop(0, i, lambda j, c: step(j, c, False), init)
    (_, l0, acc0), (_, l1, acc1) = step(i, carry, True)
    lane_o = lax.broadcasted_iota(jnp.int32, acc0.shape, 1)
    o_ref[0] = jnp.where(lane_o < dh, acc0 / l0, acc1 / l1).astype(BF16)


def _fox(fq, fk, fv, f_rows):
    bsz, seq, w = fq.shape
    dh = w // FOX_HEADS
    tq = min(ATT_TQ, seq)
    assert f_rows.shape[-1] == tq, "diagonal block masking assumes tq == tk"
    tk = tq
    return pl.pallas_call(
        functools.partial(_fox_kernel, tk=tk, dh=dh),
        grid=(bsz, FOX_HEADS // 2, seq // tq),
        in_specs=[pl.BlockSpec((1, tq, 2 * dh), lambda b, h, i: (b, i, h)),
                  pl.BlockSpec((1, seq, 2 * dh), lambda b, h, i: (b, 0, h)),
                  pl.BlockSpec((1, seq, 2 * dh), lambda b, h, i: (b, 0, h)),
                  pl.BlockSpec((1, 1, 2, seq // tk, tk), lambda b, h, i: (b, h, 0, 0, 0))],
        out_specs=pl.BlockSpec((1, tq, 2 * dh), lambda b, h, i: (b, i, h)),
        out_shape=jax.ShapeDtypeStruct((bsz, seq, w), BF16),
        compiler_params=_params("parallel", "parallel", "arbitrary"),
        name="fox_attention",
    )(fq, fk, fv, f_rows)


def _merge_kernel(x_ref, ya_ref, yb_ref, yc_ref, wgate_ref, wbr_ref, wout_ref,
                  g_ref, b_ref, o_ref):
    x = x_ref[...]
    d = x.shape[-1]
    logits = jnp.dot(x.astype(BF16), wgate_ref[...], preferred_element_type=F32)
    merged = None
    for n, y_ref in enumerate((ya_ref, yb_ref, yc_ref)):
        proj = jnp.dot(y_ref[...], wbr_ref[n], preferred_element_type=F32)
        term = jax.nn.sigmoid(logits[:, n * d:(n + 1) * d]) * proj
        merged = term if merged is None else merged + term
    mix = jnp.dot(merged.astype(BF16), wout_ref[...], preferred_element_type=F32)
    o_ref[...] = _layer_norm(DEEPNORM_ALPHA * x + mix, g_ref[...], b_ref[...])


def _merge(x2, ya, yb, yc, w_gate, w_branch, w_out, g, b):
    n, d = x2.shape
    bw = ya.shape[-1]
    tm = min(MERGE_TM, n)
    tok = lambda w: pl.BlockSpec((tm, w), lambda i: (i, 0))
    return pl.pallas_call(
        _merge_kernel,
        grid=(n // tm,),
        in_specs=[tok(d), tok(bw), tok(bw), tok(bw),
                  _const_spec((d, N_BRANCH * d)), _const_spec((N_BRANCH, bw, d)),
                  _const_spec((d, d)), _const_spec((1, d)), _const_spec((1, d))],
        out_specs=tok(d),
        out_shape=jax.ShapeDtypeStruct((n, d), F32),
        compiler_params=_params("parallel"),
        name="merge_out_ln",
    )(x2, ya, yb, yc, w_gate, w_branch, w_out, g, b)


def _ple_kernel(x_ref, p_ref, wp_ref, wg_ref, bg_ref, g_ref, b_ref, o_ref):
    x = x_ref[...]
    pe = jnp.dot(p_ref[...].astype(BF16), wp_ref[...], preferred_element_type=F32)
    gate = jax.nn.sigmoid(
        jnp.dot(x.astype(BF16), wg_ref[...], preferred_element_type=F32) + bg_ref[...])
    o_ref[...] = _layer_norm(DEEPNORM_ALPHA * x + gate * pe, g_ref[...], b_ref[...])


def _ple(x2, p2, w_proj, w_gate, b_gate, g, b):
    n, d = x2.shape
    pd = p2.shape[-1]
    tm = min(PLE_TM, n)
    tok = lambda w: pl.BlockSpec((tm, w), lambda i: (i, 0))
    return pl.pallas_call(
        _ple_kernel,
        grid=(n // tm,),
        in_specs=[tok(d), tok(pd), _const_spec((pd, d)), _const_spec((d, d)),
                  _const_spec((1, d)), _const_spec((1, d)), _const_spec((1, d))],
        out_specs=tok(d),
        out_shape=jax.ShapeDtypeStruct((n, d), F32),
        compiler_params=_params("parallel"),
        name="ple_ln",
    )(x2, p2, w_proj, w_gate, b_gate, g, b)


def _block_diag(w):
    nb, c, _ = w.shape
    eye = jnp.eye(nb, dtype=w.dtype)
    return (eye[:, None, :, None] * w[:, :, None, :]).reshape(nb * c, nb * c)


def _row(v):
    return v.reshape(1, -1).astype(F32)


def kernel(x, p, ffn1_w_up, ffn1_w_down, ln1_g, ln1_b, w_in, conv_w, conv_b, lru_wa, lru_ba, lru_wx, lru_bx, lru_lambda, gla_w_g2, gla_b_g, gla_norm_g, fox_b_f, w_branch, w_out, ln2_g, ln2_b, ffn2_w_up, ffn2_w_down, ln3_g, ln3_b, ple_w_proj, ple_w_gate, ple_b_gate, ln4_g, ln4_b):
    bsz, seq, d = x.shape
    depth = w_in.shape[0]
    dff = ffn1_w_down.shape[1]
    lru_w = conv_w.shape[-1]
    dk_all = gla_w_g2.shape[-1]
    dv_all = gla_norm_g.shape[-1]
    fox_w = w_branch.shape[2]
    n_low = gla_w_g2.shape[1]
    n_fh = fox_b_f.shape[-1]
    dh = fox_w // n_fh

    sizes = (lru_w, lru_w, dk_all, dk_all, dv_all, n_low, dv_all, fox_w, fox_w, fox_w,
             n_fh, N_BRANCH * d)
    offs = [0]
    for s in sizes:
        offs.append(offs[-1] + s)
    assert offs[-1] == w_in.shape[-1]
    tk = min(ATT_TK, seq)

    mask = (jnp.arange(dv_all)[:, None] // (dv_all // GLA_HEADS)
            == jnp.arange(dk_all)[None, :] // (dk_all // GLA_HEADS)).astype(F32)

    x2 = x.reshape(bsz * seq, d)
    for i in range(depth):
        wi = w_in[i]
        col = lambda a: wi[:, offs[a]:offs[a + 1]]
        small_w = jnp.zeros((d, LANES), F32)
        small_w = small_w.at[:, :n_low].set(col(5)).at[:, n_low:n_low + n_fh].set(col(10))
        w_proj = jnp.concatenate(
            [col(0), col(1), col(2), col(3), col(4), col(6),
             col(7) * (dh ** -0.5), col(8), col(9), small_w], axis=1).astype(BF16)
        bf_pad = jnp.zeros((1, LANES), F32).at[0, n_low:n_low + n_fh].set(fox_b_f[i])
        w_g2_pad = jnp.zeros((LANES, dk_all), F32).at[:n_low].set(gla_w_g2[i]).astype(BF16)
        w_lru_gates = jnp.concatenate(
            [_block_diag(lru_wa[i]), _block_diag(lru_wx[i])], axis=1).astype(BF16)

        x2 = _ffn(x2, ffn1_w_up[i][:, :dff].astype(BF16), ffn1_w_up[i][:, dff:].astype(BF16),
                  ffn1_w_down[i].astype(BF16), _row(ln1_g[i]), _row(ln1_b[i]))

        lru_in, gla_in, fq, fk, fv, small, fcum = _proj(
            x2.reshape(bsz, seq, d), w_proj, bf_pad,
            (2 * lru_w, 2 * dk_all + 2 * dv_all, fox_w))
        ya = _lru(lru_in, conv_w[i].astype(F32), _row(conv_b[i]), w_lru_gates,
                  _row(lru_ba[i]), _row(lru_bx[i]), _row(lru_lambda[i]))
        yb = _gla(gla_in, small, w_g2_pad, _row(gla_b_g[i]), _row(gla_norm_g[i]), mask,
                  dk_all, dv_all)
        f_rows = fcum[:, :, n_low:n_low + n_fh].transpose(0, 2, 1).reshape(
            bsz, n_fh // 2, 2, seq // tk, tk)
        yc = _fox(fq, fk, fv, f_rows)
        n = bsz * seq
        x2 = _merge(x2, ya.reshape(n, -1), yb.reshape(n, -1), yc.reshape(n, -1),
                    col(11).astype(BF16), w_branch[i].astype(BF16), w_out[i].astype(BF16),
                    _row(ln2_g[i]), _row(ln2_b[i]))

        x2 = _ffn(x2, ffn2_w_up[i][:, :dff].astype(BF16), ffn2_w_up[i][:, dff:].astype(BF16),
                  ffn2_w_down[i].astype(BF16), _row(ln3_g[i]), _row(ln3_b[i]))

        x2 = _ple(x2, p[i].reshape(n, -1), ple_w_proj[i].astype(BF16),
                  ple_w_gate[i].astype(BF16), _row(ple_b_gate[i]),
                  _row(ln4_g[i]), _row(ln4_b[i]))
    return x2.reshape(bsz, seq, d)
```

```python
import functools

import jax
import jax.numpy as jnp
import numpy as np
from jax import lax
from jax.experimental import pallas as pl
from jax.experimental.pallas import tpu as pltpu

F32 = jnp.float32
BF16 = jnp.bfloat16

DEPTH = 2
CHUNK = 64
LRU_BLOCKS = 8
CONV_WIDTH = 4
LRU_C = 8.0
GLA_HEADS = 4
GLA_LOWRANK = 16
GLA_TAU = 16.0
FOX_HEADS = 8
N_BRANCH = 3
LN_EPS = 1e-5
RMS_EPS = 1e-6
DEEPNORM_ALPHA = (2 * DEPTH) ** 0.25

LOG2E = 1.4426950408889634
F_PIECES = 3

LANES = 128
SUBLANES = 8
VMEM_LIMIT = 56 * 1024 * 1024

FFN_TM = 512
PROJ_TM = 512
LRU_T = 512
GLA_T = 512
ATT_TQ = 512
MERGE_TM = 512
PLE_TM = 512


def _const_spec(shape):
    nd = len(shape)
    return pl.BlockSpec(shape, lambda *_: (0,) * nd, pipeline_mode=pl.Buffered(1))


def _params(*sem):
    return pltpu.CompilerParams(dimension_semantics=sem, vmem_limit_bytes=VMEM_LIMIT)


def _layer_norm(z, g, b):
    mu = jnp.mean(z, axis=-1, keepdims=True)
    d = z - mu
    var = jnp.mean(d * d, axis=-1, keepdims=True)
    return d * lax.rsqrt(var + LN_EPS) * g + b


def _softplus(z):
    return jnp.maximum(z, 0.0) + jnp.log1p(jnp.exp(-jnp.abs(z)))


def _log_sigmoid(z):
    return jnp.minimum(z, 0.0) - jnp.log1p(jnp.exp(-jnp.abs(z)))


def _silu(z):
    return z * jax.nn.sigmoid(z)


def _cumsum_rows(x, period):
    rows = x.shape[0]
    pos = lax.broadcasted_iota(jnp.int32, x.shape, 0) % period
    s = 1
    while s < period:
        x = x + jnp.where(pos >= s, pltpu.roll(x, s, axis=0), 0.0)
        s *= 2
    del rows
    return x


def _ffn_kernel(x_ref, wg_ref, wu_ref, wd_ref, g_ref, b_ref, o_ref):
    x = x_ref[...]
    xb = x.astype(BF16)
    gate = jnp.dot(xb, wg_ref[...], preferred_element_type=F32)
    up = jnp.dot(xb, wu_ref[...], preferred_element_type=F32)
    act = (_silu(gate) * up).astype(BF16)
    y = jnp.dot(act, wd_ref[...], preferred_element_type=F32)
    o_ref[...] = _layer_norm(DEEPNORM_ALPHA * x + 0.5 * y, g_ref[...], b_ref[...])


def _ffn(x2, w_gate, w_up, w_down, g, b):
    n, d = x2.shape
    dff = w_gate.shape[1]
    tm = min(FFN_TM, n)
    return pl.pallas_call(
        _ffn_kernel,
        grid=(n // tm,),
        in_specs=[pl.BlockSpec((tm, d), lambda i: (i, 0)),
                  _const_spec((d, dff)), _const_spec((d, dff)), _const_spec((dff, d)),
                  _const_spec((1, d)), _const_spec((1, d))],
        out_specs=pl.BlockSpec((tm, d), lambda i: (i, 0)),
        out_shape=jax.ShapeDtypeStruct((n, d), F32),
        compiler_params=_params("parallel"),
        name="ffn_ln",
    )(x2, w_gate, w_up, w_down, g, b)


def _proj_kernel(x_ref, w_ref, bf_ref, place_ref, qaug_ref, lru_ref, gla_ref, fq_ref,
                 fk_ref, fv_ref, small_ref, carry_ref, *, widths, f_lanes):
    @pl.when(pl.program_id(1) == 0)
    def _():
        carry_ref[...] = jnp.zeros_like(carry_ref)

    xb = x_ref[0].astype(BF16)
    y = jnp.dot(xb, w_ref[...], preferred_element_type=F32)
    rows = y.shape[0]
    w_lru, w_gla, w_fox = widths
    o = 0
    lru_ref[0] = y[:, o:o + w_lru].astype(BF16); o += w_lru
    gla_ref[0] = y[:, o:o + w_gla].astype(BF16); o += w_gla
    yq = y[:, o:o + w_fox]; o += w_fox
    yk = y[:, o:o + w_fox]; o += w_fox
    fv_ref[0] = y[:, o:o + w_fox].astype(BF16); o += w_fox
    small = y[:, o:o + LANES]
    small_ref[0] = small

    lane = lax.broadcasted_iota(jnp.int32, small.shape, 1)
    is_f = jnp.logical_and(lane >= f_lanes[0], lane < f_lanes[1])
    log_f = jnp.where(is_f, _log_sigmoid(small + bf_ref[...]), 0.0)
    f_cum = _cumsum_rows(log_f, rows) + carry_ref[0:1, :]
    carry_ref[0:1, :] = f_cum[rows - 1:, :]
    f2 = f_cum * LOG2E
    hi = f2.astype(BF16)
    r1 = f2 - hi.astype(F32)
    mid = r1.astype(BF16)
    lo = (r1 - mid.astype(F32)).astype(BF16)
    bias = jnp.dot(jnp.concatenate([hi, mid, lo], axis=1), place_ref[...],
                   preferred_element_type=F32).astype(BF16)

    qaug = jnp.broadcast_to(qaug_ref[...], (rows, LANES)).astype(BF16)
    for pr in range(w_fox // LANES):
        src = slice(pr * LANES, (pr + 1) * LANES)
        fq_ref[0, :, 2 * pr * LANES:(2 * pr + 1) * LANES] = yq[:, src].astype(BF16)
        fq_ref[0, :, (2 * pr + 1) * LANES:(2 * pr + 2) * LANES] = qaug
        fk_ref[0, :, 2 * pr * LANES:(2 * pr + 1) * LANES] = yk[:, src].astype(BF16)
        fk_ref[0, :, (2 * pr + 1) * LANES:(2 * pr + 2) * LANES] = bias[:, src]


def _proj(x, w_proj, bf_pad, place, qaug, widths, f_lanes):
    bsz, seq, d = x.shape
    tm = min(PROJ_TM, seq)
    w_lru, w_gla, w_fox = widths
    n_out = w_proj.shape[1]
    tok = lambda w: pl.BlockSpec((1, tm, w), lambda b, i: (b, i, 0))
    return pl.pallas_call(
        functools.partial(_proj_kernel, widths=widths, f_lanes=f_lanes),
        grid=(bsz, seq // tm),
        in_specs=[tok(d), _const_spec((d, n_out)), _const_spec((1, LANES)),
                  _const_spec(place.shape), _const_spec((1, LANES))],
        out_specs=[tok(w_lru), tok(w_gla), tok(2 * w_fox), tok(2 * w_fox), tok(w_fox),
                   tok(LANES)],
        out_shape=[jax.ShapeDtypeStruct((bsz, seq, w_lru), BF16),
                   jax.ShapeDtypeStruct((bsz, seq, w_gla), BF16),
                   jax.ShapeDtypeStruct((bsz, seq, 2 * w_fox), BF16),
                   jax.ShapeDtypeStruct((bsz, seq, 2 * w_fox), BF16),
                   jax.ShapeDtypeStruct((bsz, seq, w_fox), BF16),
                   jax.ShapeDtypeStruct((bsz, seq, LANES), F32)],
        scratch_shapes=[pltpu.VMEM((8, LANES), F32)],
        compiler_params=_params("parallel", "arbitrary"),
        name="mixer_in_proj",
    )(x, w_proj, bf_pad, place, qaug)


def _lru_kernel(in_ref, cw_ref, cb_ref, wg_ref, ba_ref, bx_ref, lam_ref, o_ref,
                halo_ref, h_ref, *, width):
    @pl.when(pl.program_id(1) == 0)
    def _():
        halo_ref[...] = jnp.zeros_like(halo_ref)
        h_ref[...] = jnp.zeros_like(h_ref)

    tile = in_ref[0]
    ax = tile[:, :width].astype(F32)
    ay = tile[:, width:].astype(F32)
    rows = ax.shape[0]

    ext = jnp.concatenate([halo_ref[...], ax], axis=0)
    u = cb_ref[...] + cw_ref[CONV_WIDTH - 1:CONV_WIDTH, :] * ax
    for k in range(CONV_WIDTH - 1):
        shift = CONV_WIDTH - 1 - k
        u = u + cw_ref[k:k + 1, :] * ext[8 - shift:8 - shift + rows, :]
    halo_ref[...] = ax[rows - 8:, :]

    gates = jnp.dot(u.astype(BF16), wg_ref[...], preferred_element_type=F32)
    r = jax.nn.sigmoid(gates[:, :width] + ba_ref[...])
    i = jax.nn.sigmoid(gates[:, width:] + bx_ref[...])
    log_a = (-LRU_C) * r * _softplus(-lam_ref[...])
    a = jnp.exp(log_a)
    mult = jnp.sqrt(-jnp.tanh(log_a) * (1.0 + a * a))
    bterm = mult * (i * u)

    groups = rows // SUBLANES
    a3 = a.reshape(groups, SUBLANES, width)
    b3 = bterm.reshape(groups, SUBLANES, width)
    pos = lax.broadcasted_iota(jnp.int32, a3.shape, 1)
    s = 1
    while s < SUBLANES:
        keep = pos >= s
        a_sh = jnp.where(keep, pltpu.roll(a3, s, axis=1), 1.0)
        b_sh = jnp.where(keep, pltpu.roll(b3, s, axis=1), 0.0)
        b3 = b3 + a3 * b_sh
        a3 = a3 * a_sh
        s *= 2
    h_prev = h_ref[0:1, :]
    hs = []
    for gi in range(groups):
        hg = b3[gi] + a3[gi] * h_prev
        hs.append(hg)
        h_prev = hg[SUBLANES - 1:, :]
    h_ref[0:1, :] = h_prev
    h = jnp.concatenate(hs, axis=0)
    o_ref[0] = (jax.nn.gelu(ay) * h).astype(BF16)


def _lru(lru_in, conv_w, conv_b, w_gates, ba, bx, lam):
    bsz, seq, w2 = lru_in.shape
    width = w2 // 2
    t = min(LRU_T, seq)
    return pl.pallas_call(
        functools.partial(_lru_kernel, width=width),
        grid=(bsz, seq // t),
        in_specs=[pl.BlockSpec((1, t, w2), lambda b, i: (b, i, 0)),
                  _const_spec((CONV_WIDTH, width)), _const_spec((1, width)),
                  _const_spec((width, w2)), _const_spec((1, width)),
                  _const_spec((1, width)), _const_spec((1, width))],
        out_specs=pl.BlockSpec((1, t, width), lambda b, i: (b, i, 0)),
        out_shape=jax.ShapeDtypeStruct((bsz, seq, width), BF16),
        scratch_shapes=[pltpu.VMEM((8, width), F32), pltpu.VMEM((8, width), F32)],
        compiler_params=_params("parallel", "arbitrary"),
        name="rg_lru",
    )(lru_in, conv_w, conv_b, w_gates, ba, bx, lam)


def _gla_kernel(in_ref, small_ref, wg2_ref, bg_ref, ng_ref, mask_ref, o_ref, st_ref,
                *, dk_all, dv_all):
    @pl.when(pl.program_id(1) == 0)
    def _():
        st_ref[...] = jnp.zeros_like(st_ref)

    rows = in_ref.shape[1]
    dv = dv_all // GLA_HEADS
    dk = dk_all // GLA_HEADS
    low = small_ref[0].astype(BF16)
    g = jnp.dot(low, wg2_ref[...], preferred_element_type=F32) + bg_ref[...]
    log_alpha = _log_sigmoid(g) * (1.0 / GLA_TAU)
    g_cum = _cumsum_rows(log_alpha, CHUNK)

    for c in range(rows // CHUNK):
        sl = slice(c * CHUNK, (c + 1) * CHUNK)
        blk = in_ref[0, sl, :]
        q = (blk[:, :dk_all].astype(F32) * (dk ** -0.5)).astype(BF16)
        k = blk[:, dk_all:2 * dk_all].astype(F32)
        v = blk[:, 2 * dk_all:2 * dk_all + dv_all]
        gate = blk[:, 2 * dk_all + dv_all:].astype(F32)
        gc = g_cum[sl, :]
        g_tot = gc[CHUNK - 1:, :]
        k_dec = (k * jnp.exp(g_tot - gc)).astype(BF16)
        delta = lax.dot_general(v, k_dec, (((0,), (0,)), ((), ())),
                                preferred_element_type=F32)
        st = st_ref[...] * jnp.exp(g_tot) + delta * mask_ref[...]
        st_ref[...] = st
        o = lax.dot_general(q, st.astype(BF16), (((1,), (1,)), ((), ())),
                            preferred_element_type=F32)
        outs = []
        for h in range(GLA_HEADS):
            oh = o[:, h * dv:(h + 1) * dv]
            ms = jnp.mean(oh * oh, axis=-1, keepdims=True)
            outs.append(oh * lax.rsqrt(ms + RMS_EPS))
        o = jnp.concatenate(outs, axis=-1) * ng_ref[...]
        o_ref[0, sl, :] = (o * _silu(gate)).astype(BF16)


def _gla(gla_in, small, w_g2_pad, b_g, norm_g, mask, dk_all, dv_all):
    bsz, seq, w = gla_in.shape
    t = min(GLA_T, seq)
    return pl.pallas_call(
        functools.partial(_gla_kernel, dk_all=dk_all, dv_all=dv_all),
        grid=(bsz, seq // t),
        in_specs=[pl.BlockSpec((1, t, w), lambda b, i: (b, i, 0)),
                  pl.BlockSpec((1, t, LANES), lambda b, i: (b, i, 0)),
                  _const_spec((LANES, dk_all)), _const_spec((1, dk_all)),
                  _const_spec((1, dv_all)), _const_spec((dv_all, dk_all))],
        out_specs=pl.BlockSpec((1, t, dv_all), lambda b, i: (b, i, 0)),
        out_shape=jax.ShapeDtypeStruct((bsz, seq, dv_all), BF16),
        scratch_shapes=[pltpu.VMEM((dv_all, dk_all), F32)],
        compiler_params=_params("parallel", "arbitrary"),
        name="gla",
    )(gla_in, small, w_g2_pad, b_g, norm_g, mask)


def _fox_kernel(q_ref, k_ref, v_ref, o_ref, qt_ref, sa_ref, sb_ref, m_ref, acc_ref,
                *, tk, dh):
    tq = q_ref.shape[1]
    i = pl.program_id(2)
    q = q_ref[0]
    lane = lax.broadcasted_iota(jnp.int32, q.shape, 1)
    zero = jnp.zeros_like(q)
    for hh in range(2):
        own = jnp.logical_or(
            jnp.logical_and(lane >= hh * dh, lane < (hh + 1) * dh),
            jnp.logical_and(lane >= LANES + hh * F_PIECES, lane < LANES + (hh + 1) * F_PIECES))
        qt_ref[hh] = jnp.where(own, q, zero).T
    m_ref[...] = jnp.full(m_ref.shape, -jnp.inf, F32)
    acc_ref[...] = jnp.zeros(acc_ref.shape, F32)
    vlane = lax.broadcasted_iota(jnp.int32, (tk, LANES), 1)
    ones = jnp.ones((tk, LANES), BF16)

    def scores(j, s_ref):
        off = pl.multiple_of(j * tk, tk)
        kt = k_ref[0, pl.ds(off, tk), :]
        for hh in range(2):
            s_ref[hh] = jnp.dot(kt, qt_ref[hh], preferred_element_type=F32)

    def consume(j, s_ref, masked):
        off = pl.multiple_of(j * tk, tk)
        vt = v_ref[0, pl.ds(off, tk), :]
        for hh in range(2):
            s = s_ref[hh]
            if masked:
                kpos = j * tk + lax.broadcasted_iota(jnp.int32, s.shape, 0)
                qpos = i * tq + lax.broadcasted_iota(jnp.int32, s.shape, 1)
                s = jnp.where(kpos <= qpos, s, -jnp.inf)
            m = m_ref[hh]
            m_new = jnp.maximum(m, jnp.max(s, axis=0, keepdims=True))
            alpha = jnp.exp2(m - m_new)
            p = jnp.exp2(s - m_new).astype(BF16)
            own_v = jnp.logical_and(vlane >= hh * dh, vlane < (hh + 1) * dh)
            vh = jnp.where(own_v, vt, ones)
            pv = lax.dot_general(vh, p, (((0,), (0,)), ((), ())),
                                 preferred_element_type=F32)
            acc_ref[hh] = alpha * acc_ref[hh] + pv
            m_ref[hh] = m_new

    def body(t, _):
        j = 2 * t
        scores(j + 1, sb_ref)
        consume(j, sa_ref, False)
        scores(j + 2, sa_ref)
        consume(j + 1, sb_ref, False)
        return 0

    assert tq == 2 * tk
    scores(0, sa_ref)
    lax.fori_loop(0, i, body, 0)
    j = 2 * i
    scores(j + 1, sb_ref)
    consume(j, sa_ref, True)
    consume(j + 1, sb_ref, True)
    acc0 = acc_ref[0]
    acc1 = acc_ref[1]
    o_ref[0] = jnp.concatenate(
        [acc0[:dh] / acc0[dh:dh + 1], acc1[dh:] / acc1[0:1]], axis=0).astype(BF16)


def _fox(fq, fk, fv):
    bsz, seq, w = fv.shape
    dh = w // FOX_HEADS
    tq = min(ATT_TQ, seq)
    tk = tq // 2
    assert 2 * dh == LANES and fq.shape[-1] == 2 * w
    return pl.pallas_call(
        functools.partial(_fox_kernel, tk=tk, dh=dh),
        grid=(bsz, FOX_HEADS // 2, seq // tq),
        in_specs=[pl.BlockSpec((1, tq, 2 * LANES), lambda b, h, i: (b, i, h)),
                  pl.BlockSpec((1, seq, 2 * LANES), lambda b, h, i: (b, 0, h)),
                  pl.BlockSpec((1, seq, LANES), lambda b, h, i: (b, 0, h))],
        out_specs=pl.BlockSpec((1, LANES, tq), lambda b, h, i: (b, h, i)),
        out_shape=jax.ShapeDtypeStruct((bsz, w, seq), BF16),
        scratch_shapes=[pltpu.VMEM((2, 2 * LANES, tq), BF16),
                        pltpu.VMEM((2, tk, tq), F32), pltpu.VMEM((2, tk, tq), F32),
                        pltpu.VMEM((2, 1, tq), F32), pltpu.VMEM((2, LANES, tq), F32)],
        compiler_params=_params("parallel", "parallel", "arbitrary"),
        name="fox_attention",
    )(fq, fk, fv)


def _merge_kernel(x_ref, ya_ref, yb_ref, yc_ref, wgate_ref, wbr_ref, wout_ref,
                  g_ref, b_ref, o_ref):
    x = x_ref[...]
    d = x.shape[-1]
    logits = jnp.dot(x.astype(BF16), wgate_ref[...], preferred_element_type=F32)
    merged = None
    for n in range(N_BRANCH):
        if n < 2:
            y = (ya_ref, yb_ref)[n][...]
            proj = jnp.dot(y, wbr_ref[n], preferred_element_type=F32)
        else:
            proj = lax.dot_general(yc_ref[0], wbr_ref[n], (((0,), (0,)), ((), ())),
                                   preferred_element_type=F32)
        term = jax.nn.sigmoid(logits[:, n * d:(n + 1) * d]) * proj
        merged = term if merged is None else merged + term
    mix = jnp.dot(merged.astype(BF16), wout_ref[...], preferred_element_type=F32)
    o_ref[...] = _layer_norm(DEEPNORM_ALPHA * x + mix, g_ref[...], b_ref[...])


def _merge(x2, ya, yb, yc_t, w_gate, w_branch, w_out, g, b):
    n, d = x2.shape
    bw = ya.shape[-1]
    seq = yc_t.shape[-1]
    tm = min(MERGE_TM, seq)
    per_seq = seq // tm
    tok = lambda w: pl.BlockSpec((tm, w), lambda i: (i, 0))
    return pl.pallas_call(
        _merge_kernel,
        grid=(n // tm,),
        in_specs=[tok(d), tok(bw), tok(bw),
                  pl.BlockSpec((1, bw, tm), lambda i: (i // per_seq, 0, i % per_seq)),
                  _const_spec((d, N_BRANCH * d)), _const_spec((N_BRANCH, bw, d)),
                  _const_spec((d, d)), _const_spec((1, d)), _const_spec((1, d))],
        out_specs=tok(d),
        out_shape=jax.ShapeDtypeStruct((n, d), F32),
        compiler_params=_params("parallel"),
        name="merge_out_ln",
    )(x2, ya, yb, yc_t, w_gate, w_branch, w_out, g, b)


def _ple_kernel(x_ref, p_ref, wp_ref, wg_ref, bg_ref, g_ref, b_ref, o_ref):
    x = x_ref[...]
    pe = jnp.dot(p_ref[...].astype(BF16), wp_ref[...], preferred_element_type=F32)
    gate = jax.nn.sigmoid(
        jnp.dot(x.astype(BF16), wg_ref[...], preferred_element_type=F32) + bg_ref[...])
    o_ref[...] = _layer_norm(DEEPNORM_ALPHA * x + gate * pe, g_ref[...], b_ref[...])


def _ple(x2, p2, w_proj, w_gate, b_gate, g, b):
    n, d = x2.shape
    pd = p2.shape[-1]
    tm = min(PLE_TM, n)
    tok = lambda w: pl.BlockSpec((tm, w), lambda i: (i, 0))
    return pl.pallas_call(
        _ple_kernel,
        grid=(n // tm,),
        in_specs=[tok(d), tok(pd), _const_spec((pd, d)), _const_spec((d, d)),
                  _const_spec((1, d)), _const_spec((1, d)), _const_spec((1, d))],
        out_specs=tok(d),
        out_shape=jax.ShapeDtypeStruct((n, d), F32),
        compiler_params=_params("parallel"),
        name="ple_ln",
    )(x2, p2, w_proj, w_gate, b_gate, g, b)


def _block_diag(w):
    nb, c, _ = w.shape
    eye = jnp.eye(nb, dtype=w.dtype)
    return (eye[:, None, :, None] * w[:, :, None, :]).reshape(nb * c, nb * c)


def _row(v):
    return v.reshape(1, -1).astype(F32)


def kernel(x, p, ffn1_w_up, ffn1_w_down, ln1_g, ln1_b, w_in, conv_w, conv_b, lru_wa, lru_ba, lru_wx, lru_bx, lru_lambda, gla_w_g2, gla_b_g, gla_norm_g, fox_b_f, w_branch, w_out, ln2_g, ln2_b, ffn2_w_up, ffn2_w_down, ln3_g, ln3_b, ple_w_proj, ple_w_gate, ple_b_gate, ln4_g, ln4_b):
    bsz, seq, d = x.shape
    depth = w_in.shape[0]
    dff = ffn1_w_down.shape[1]
    lru_w = conv_w.shape[-1]
    dk_all = gla_w_g2.shape[-1]
    dv_all = gla_norm_g.shape[-1]
    fox_w = w_branch.shape[2]
    n_low = gla_w_g2.shape[1]
    n_fh = fox_b_f.shape[-1]
    dh = fox_w // n_fh

    sizes = (lru_w, lru_w, dk_all, dk_all, dv_all, n_low, dv_all, fox_w, fox_w, fox_w,
             n_fh, N_BRANCH * d)
    offs = [0]
    for s in sizes:
        offs.append(offs[-1] + s)
    assert offs[-1] == w_in.shape[-1]
    place = np.zeros((F_PIECES * LANES, (n_fh // 2) * LANES), np.float32)
    for h in range(n_fh):
        for r in range(F_PIECES):
            place[r * LANES + n_low + h, (h // 2) * LANES + (h % 2) * F_PIECES + r] = 1.0
    place = jnp.asarray(place, BF16)
    qaug = jnp.where(jnp.arange(LANES) < 2 * F_PIECES, -1.0, 0.0).astype(F32).reshape(1, LANES)

    mask =(jnp.arange(dv_all)[:, None] // (dv_all // GLA_HEADS)
            == jnp.arange(dk_all)[None, :] // (dk_all // GLA_HEADS)).astype(F32)

    x2 = x.reshape(bsz * seq, d)
    for i in range(depth):
        wi = w_in[i]
        col = lambda a: wi[:, offs[a]:offs[a + 1]]
        small_w = jnp.zeros((d, LANES), F32)
        small_w = small_w.at[:, :n_low].set(col(5)).at[:, n_low:n_low + n_fh].set(col(10))
        w_proj = jnp.concatenate(
            [col(0), col(1), col(2), col(3), col(4), col(6),
             col(7) * (dh ** -0.5 * LOG2E), col(8), col(9), small_w], axis=1).astype(BF16)
        bf_pad = jnp.zeros((1, LANES), F32).at[0, n_low:n_low + n_fh].set(fox_b_f[i])
        w_g2_pad = jnp.zeros((LANES, dk_all), F32).at[:n_low].set(gla_w_g2[i]).astype(BF16)
        w_lru_gates = jnp.concatenate(
            [_block_diag(lru_wa[i]), _block_diag(lru_wx[i])], axis=1).astype(BF16)

        x2 = _ffn(x2, ffn1_w_up[i][:, :dff].astype(BF16), ffn1_w_up[i][:, dff:].astype(BF16),
                  ffn1_w_down[i].astype(BF16), _row(ln1_g[i]), _row(ln1_b[i]))

        lru_in, gla_in, fq, fk, fv, small = _proj(
            x2.reshape(bsz, seq, d), w_proj, bf_pad, place, qaug,
            (2 * lru_w, 2 * dk_all + 2 * dv_all, fox_w), (n_low, n_low + n_fh))
        ya = _lru(lru_in, conv_w[i].astype(F32), _row(conv_b[i]), w_lru_gates,
                  _row(lru_ba[i]), _row(lru_bx[i]), _row(lru_lambda[i]))
        yb = _gla(gla_in, small, w_g2_pad, _row(gla_b_g[i]), _row(gla_norm_g[i]), mask,
                  dk_all, dv_all)
        yc_t = _fox(fq, fk, fv)
        n = bsz * seq
        x2 = _merge(x2, ya.reshape(n, -1), yb.reshape(n, -1), yc_t,
                    col(11).astype(BF16), w_branch[i].astype(BF16), w_out[i].astype(BF16),
                    _row(ln2_g[i]), _row(ln2_b[i]))

        x2 = _ffn(x2, ffn2_w_up[i][:, :dff].astype(BF16), ffn2_w_up[i][:, dff:].astype(BF16),
                  ffn2_w_down[i].astype(BF16), _row(ln3_g[i]), _row(ln3_b[i]))

        x2 = _ple(x2, p[i].reshape(n, -1), ple_w_proj[i].astype(BF16),
                  ple_w_gate[i].astype(BF16), _row(ple_b_gate[i]),
                  _row(ln4_g[i]), _row(ln4_b[i]))
    return x2.reshape(bsz, seq, d)
```

```python
import functools

import jax
import jax.numpy as jnp
import numpy as np
from jax import lax
from jax.experimental import pallas as pl
from jax.experimental.pallas import tpu as pltpu

F32 = jnp.float32
BF16 = jnp.bfloat16

DEPTH = 2
CHUNK = 64
LRU_BLOCKS = 8
CONV_WIDTH = 4
LRU_C = 8.0
GLA_HEADS = 4
GLA_LOWRANK = 16
GLA_TAU = 16.0
FOX_HEADS = 8
N_BRANCH = 3
LN_EPS = 1e-5
RMS_EPS = 1e-6
DEEPNORM_ALPHA = (2 * DEPTH) ** 0.25

LOG2E = 1.4426950408889634
F_PIECES = 3
PRUNE_BITS = 150.0
PRUNE_SLACK = 1.05

LANES = 128
SUBLANES = 8
VMEM_LIMIT = 56 * 1024 * 1024

FFN_TM = 512
PROJ_TM = 512
LRU_T = 512
GLA_T = 512
ATT_TQ = 1024
MERGE_TM = 512
PLE_TM = 512


def _const_spec(shape):
    nd = len(shape)
    return pl.BlockSpec(shape, lambda *_: (0,) * nd, pipeline_mode=pl.Buffered(1))


def _params(*sem):
    return pltpu.CompilerParams(dimension_semantics=sem, vmem_limit_bytes=VMEM_LIMIT)


def _layer_norm(z, g, b):
    mu = jnp.mean(z, axis=-1, keepdims=True)
    d = z - mu
    var = jnp.mean(d * d, axis=-1, keepdims=True)
    return d * lax.rsqrt(var + LN_EPS) * g + b


def _softplus(z):
    return jnp.maximum(z, 0.0) + jnp.log1p(jnp.exp(-jnp.abs(z)))


def _log_sigmoid(z):
    return jnp.minimum(z, 0.0) - jnp.log1p(jnp.exp(-jnp.abs(z)))


def _silu(z):
    return z * jax.nn.sigmoid(z)


def _cumsum_rows(x, period):
    rows = x.shape[0]
    pos = lax.broadcasted_iota(jnp.int32, x.shape, 0) % period
    s = 1
    while s < period:
        x = x + jnp.where(pos >= s, pltpu.roll(x, s, axis=0), 0.0)
        s *= 2
    del rows
    return x


def _ffn_kernel(x_ref, wg_ref, wu_ref, wd_ref, g_ref, b_ref, o_ref):
    x = x_ref[...]
    xb = x.astype(BF16)
    gate = jnp.dot(xb, wg_ref[...], preferred_element_type=F32)
    up = jnp.dot(xb, wu_ref[...], preferred_element_type=F32)
    act = (_silu(gate) * up).astype(BF16)
    y = jnp.dot(act, wd_ref[...], preferred_element_type=F32)
    o_ref[...] = _layer_norm(DEEPNORM_ALPHA * x + 0.5 * y, g_ref[...], b_ref[...])


def _ffn(x2, w_gate, w_up, w_down, g, b):
    n, d = x2.shape
    dff = w_gate.shape[1]
    tm = min(FFN_TM, n)
    return pl.pallas_call(
        _ffn_kernel,
        grid=(n // tm,),
        in_specs=[pl.BlockSpec((tm, d), lambda i: (i, 0)),
                  _const_spec((d, dff)), _const_spec((d, dff)), _const_spec((dff, d)),
                  _const_spec((1, d)), _const_spec((1, d))],
        out_specs=pl.BlockSpec((tm, d), lambda i: (i, 0)),
        out_shape=jax.ShapeDtypeStruct((n, d), F32),
        compiler_params=_params("parallel"),
        name="ffn_ln",
    )(x2, w_gate, w_up, w_down, g, b)


def _proj_kernel(x_ref, w_ref, bf_ref, place_ref, qaug_ref, lru_ref, gla_ref, fq_ref,
                 fk_ref, fv_ref, small_ref, fend_ref, carry_ref, *, widths, f_lanes):
    @pl.when(pl.program_id(1) == 0)
    def _():
        carry_ref[...] = jnp.zeros_like(carry_ref)

    xb = x_ref[0].astype(BF16)
    y = jnp.dot(xb, w_ref[...], preferred_element_type=F32)
    rows = y.shape[0]
    w_lru, w_gla, w_fox = widths
    o = 0
    lru_ref[0] = y[:, o:o + w_lru].astype(BF16); o += w_lru
    gla_ref[0] = y[:, o:o + w_gla].astype(BF16); o += w_gla
    yq = y[:, o:o + w_fox]; o += w_fox
    yk = y[:, o:o + w_fox]; o += w_fox
    fv_ref[0] = y[:, o:o + w_fox].astype(BF16); o += w_fox
    small = y[:, o:o + LANES]
    small_ref[0] = small

    lane = lax.broadcasted_iota(jnp.int32, small.shape, 1)
    is_f = jnp.logical_and(lane >= f_lanes[0], lane < f_lanes[1])
    log_f = jnp.where(is_f, _log_sigmoid(small + bf_ref[...]), 0.0)
    f_cum = _cumsum_rows(log_f, rows) + carry_ref[0:1, :]
    carry_ref[0:1, :] = f_cum[rows - 1:, :]
    f2 = f_cum * LOG2E
    fend_ref[0, 0] = jnp.broadcast_to(f2[rows - 1:, :], (SUBLANES, LANES))
    hi = f2.astype(BF16)
    r1 = f2 - hi.astype(F32)
    mid = r1.astype(BF16)
    lo = (r1 - mid.astype(F32)).astype(BF16)
    bias = jnp.dot(jnp.concatenate([hi, mid, lo], axis=1), place_ref[...],
                   preferred_element_type=F32).astype(BF16)

    qaug = jnp.broadcast_to(qaug_ref[...], (rows, LANES)).astype(BF16)
    for pr in range(w_fox // LANES):
        src = slice(pr * LANES, (pr + 1) * LANES)
        fq_ref[0, :, 2 * pr * LANES:(2 * pr + 1) * LANES] = yq[:, src].astype(BF16)
        fq_ref[0, :, (2 * pr + 1) * LANES:(2 * pr + 2) * LANES] = qaug
        fk_ref[0, :, 2 * pr * LANES:(2 * pr + 1) * LANES] = yk[:, src].astype(BF16)
        fk_ref[0, :, (2 * pr + 1) * LANES:(2 * pr + 2) * LANES] = bias[:, src]


def _proj(x, w_proj, bf_pad, place, qaug, widths, f_lanes):
    bsz, seq, d = x.shape
    tm = min(PROJ_TM, seq)
    w_lru, w_gla, w_fox = widths
    n_out = w_proj.shape[1]
    tok = lambda w: pl.BlockSpec((1, tm, w), lambda b, i: (b, i, 0))
    return pl.pallas_call(
        functools.partial(_proj_kernel, widths=widths, f_lanes=f_lanes),
        grid=(bsz, seq // tm),
        in_specs=[tok(d), _const_spec((d, n_out)), _const_spec((1, LANES)),
                  _const_spec(place.shape), _const_spec((1, LANES))],
        out_specs=[tok(w_lru), tok(w_gla), tok(2 * w_fox), tok(2 * w_fox), tok(w_fox),
                   tok(LANES),
                   pl.BlockSpec((1, 1, SUBLANES, LANES), lambda b, i: (b, i, 0, 0))],
        out_shape=[jax.ShapeDtypeStruct((bsz, seq, w_lru), BF16),
                   jax.ShapeDtypeStruct((bsz, seq, w_gla), BF16),
                   jax.ShapeDtypeStruct((bsz, seq, 2 * w_fox), BF16),
                   jax.ShapeDtypeStruct((bsz, seq, 2 * w_fox), BF16),
                   jax.ShapeDtypeStruct((bsz, seq, w_fox), BF16),
                   jax.ShapeDtypeStruct((bsz, seq, LANES), F32),
                   jax.ShapeDtypeStruct((bsz, seq // tm, SUBLANES, LANES), F32)],
        scratch_shapes=[pltpu.VMEM((8, LANES), F32)],
        compiler_params=_params("parallel", "arbitrary"),
        name="mixer_in_proj",
    )(x, w_proj, bf_pad, place, qaug)


def _lru_kernel(in_ref, cw_ref, cb_ref, wg_ref, ba_ref, bx_ref, lam_ref, o_ref,
                halo_ref, h_ref, *, width):
    @pl.when(pl.program_id(1) == 0)
    def _():
        halo_ref[...] = jnp.zeros_like(halo_ref)
        h_ref[...] = jnp.zeros_like(h_ref)

    tile = in_ref[0]
    ax = tile[:, :width].astype(F32)
    ay = tile[:, width:].astype(F32)
    rows = ax.shape[0]

    ext = jnp.concatenate([halo_ref[...], ax], axis=0)
    u = cb_ref[...] + cw_ref[CONV_WIDTH - 1:CONV_WIDTH, :] * ax
    for k in range(CONV_WIDTH - 1):
        shift = CONV_WIDTH - 1 - k
        u = u + cw_ref[k:k + 1, :] * ext[8 - shift:8 - shift + rows, :]
    halo_ref[...] = ax[rows - 8:, :]

    gates = jnp.dot(u.astype(BF16), wg_ref[...], preferred_element_type=F32)
    r = jax.nn.sigmoid(gates[:, :width] + ba_ref[...])
    i = jax.nn.sigmoid(gates[:, width:] + bx_ref[...])
    log_a = (-LRU_C) * r * _softplus(-lam_ref[...])
    a = jnp.exp(log_a)
    mult = jnp.sqrt(-jnp.tanh(log_a) * (1.0 + a * a))
    bterm = mult * (i * u)

    groups = rows // SUBLANES
    a3 = a.reshape(groups, SUBLANES, width)
    b3 = bterm.reshape(groups, SUBLANES, width)
    pos = lax.broadcasted_iota(jnp.int32, a3.shape, 1)
    s = 1
    while s < SUBLANES:
        keep = pos >= s
        a_sh = jnp.where(keep, pltpu.roll(a3, s, axis=1), 1.0)
        b_sh = jnp.where(keep, pltpu.roll(b3, s, axis=1), 0.0)
        b3 = b3 + a3 * b_sh
        a3 = a3 * a_sh
        s *= 2
    h_prev = h_ref[0:1, :]
    hs = []
    for gi in range(groups):
        hg = b3[gi] + a3[gi] * h_prev
        hs.append(hg)
        h_prev = hg[SUBLANES - 1:, :]
    h_ref[0:1, :] = h_prev
    h = jnp.concatenate(hs, axis=0)
    o_ref[0] = (jax.nn.gelu(ay) * h).astype(BF16)


def _lru(lru_in, conv_w, conv_b, w_gates, ba, bx, lam):
    bsz, seq, w2 = lru_in.shape
    width = w2 // 2
    t = min(LRU_T, seq)
    return pl.pallas_call(
        functools.partial(_lru_kernel, width=width),
        grid=(bsz, seq // t),
        in_specs=[pl.BlockSpec((1, t, w2), lambda b, i: (b, i, 0)),
                  _const_spec((CONV_WIDTH, width)), _const_spec((1, width)),
                  _const_spec((width, w2)), _const_spec((1, width)),
                  _const_spec((1, width)), _const_spec((1, width))],
        out_specs=pl.BlockSpec((1, t, width), lambda b, i: (b, i, 0)),
        out_shape=jax.ShapeDtypeStruct((bsz, seq, width), BF16),
        scratch_shapes=[pltpu.VMEM((8, width), F32), pltpu.VMEM((8, width), F32)],
        compiler_params=_params("parallel", "arbitrary"),
        name="rg_lru",
    )(lru_in, conv_w, conv_b, w_gates, ba, bx, lam)


def _gla_kernel(in_ref, small_ref, wg2_ref, bg_ref, ng_ref, mask_ref, o_ref, st_ref,
                *, dk_all, dv_all):
    @pl.when(pl.program_id(1) == 0)
    def _():
        st_ref[...] = jnp.zeros_like(st_ref)

    rows = in_ref.shape[1]
    dv = dv_all // GLA_HEADS
    dk = dk_all // GLA_HEADS
    low = small_ref[0].astype(BF16)
    g = jnp.dot(low, wg2_ref[...], preferred_element_type=F32) + bg_ref[...]
    log_alpha = _log_sigmoid(g) * (1.0 / GLA_TAU)
    g_cum = _cumsum_rows(log_alpha, CHUNK)

    for c in range(rows // CHUNK):
        sl = slice(c * CHUNK, (c + 1) * CHUNK)
        blk = in_ref[0, sl, :]
        q = (blk[:, :dk_all].astype(F32) * (dk ** -0.5)).astype(BF16)
        k = blk[:, dk_all:2 * dk_all].astype(F32)
        v = blk[:, 2 * dk_all:2 * dk_all + dv_all]
        gate = blk[:, 2 * dk_all + dv_all:].astype(F32)
        gc = g_cum[sl, :]
        g_tot = gc[CHUNK - 1:, :]
        k_dec = (k * jnp.exp(g_tot - gc)).astype(BF16)
        delta = lax.dot_general(v, k_dec, (((0,), (0,)), ((), ())),
                                preferred_element_type=F32)
        st = st_ref[...] * jnp.exp(g_tot) + delta * mask_ref[...]
        st_ref[...] = st
        o = lax.dot_general(q, st.astype(BF16), (((1,), (1,)), ((), ())),
                            preferred_element_type=F32)
        outs = []
        for h in range(GLA_HEADS):
            oh = o[:, h * dv:(h + 1) * dv]
            ms = jnp.mean(oh * oh, axis=-1, keepdims=True)
            outs.append(oh * lax.rsqrt(ms + RMS_EPS))
        o = jnp.concatenate(outs, axis=-1) * ng_ref[...]
        o_ref[0, sl, :] = (o * _silu(gate)).astype(BF16)


def _gla(gla_in, small, w_g2_pad, b_g, norm_g, mask, dk_all, dv_all):
    bsz, seq, w = gla_in.shape
    t = min(GLA_T, seq)
    return pl.pallas_call(
        functools.partial(_gla_kernel, dk_all=dk_all, dv_all=dv_all),
        grid=(bsz, seq // t),
        in_specs=[pl.BlockSpec((1, t, w), lambda b, i: (b, i, 0)),
                  pl.BlockSpec((1, t, LANES), lambda b, i: (b, i, 0)),
                  _const_spec((LANES, dk_all)), _const_spec((1, dk_all)),
                  _const_spec((1, dv_all)), _const_spec((dv_all, dk_all))],
        out_specs=pl.BlockSpec((1, t, dv_all), lambda b, i: (b, i, 0)),
        out_shape=jax.ShapeDtypeStruct((bsz, seq, dv_all), BF16),
        scratch_shapes=[pltpu.VMEM((dv_all, dk_all), F32)],
        compiler_params=_params("parallel", "arbitrary"),
        name="gla",
    )(gla_in, small, w_g2_pad, b_g, norm_g, mask)


def _fox_kernel(q_ref, k_ref, v_ref, fb_ref, o_ref, qt_ref, sa_ref, sb_ref, m_ref,
                acc_ref, kn_ref, *, tk, dh):
    tq = q_ref.shape[1]
    seq = k_ref.shape[1]
    i = pl.program_id(2)
    q = q_ref[0]
    lane = lax.broadcasted_iota(jnp.int32, q.shape, 1)
    zero = jnp.zeros_like(q)
    for hh in range(2):
        own = jnp.logical_or(
            jnp.logical_and(lane >= hh * dh, lane < (hh + 1) * dh),
            jnp.logical_and(lane >= LANES + hh * F_PIECES, lane < LANES + (hh + 1) * F_PIECES))
        qt_ref[hh] = jnp.where(own, q, zero).T
    m_ref[...] = jnp.full(m_ref.shape, -jnp.inf, F32)
    acc_ref[...] = jnp.zeros(acc_ref.shape, F32)
    vlane = lax.broadcasted_iota(jnp.int32, (tk, LANES), 1)
    ones = jnp.ones((tk, LANES), BF16)

    @pl.when(i == 0)
    def _():
        def kn_body(c, mx):
            kt = k_ref[0, pl.ds(pl.multiple_of(c * tk, tk), tk), 0:LANES].astype(F32)
            sq = kt * kt
            out = []
            for hh in range(2):
                own_k = jnp.logical_and(vlane >= hh * dh, vlane < (hh + 1) * dh)
                n2 = jnp.sum(jnp.where(own_k, sq, 0.0), axis=1, keepdims=True)
                out.append(jnp.maximum(mx[hh], jnp.max(n2, axis=0, keepdims=True)))
            return tuple(out)
        mx = lax.fori_loop(0, seq // tk, kn_body,
                           (jnp.zeros((1, 1), F32), jnp.zeros((1, 1), F32)))
        for hh in range(2):
            kn_ref[hh:hh + 1, :] = jnp.broadcast_to(mx[hh], (1, LANES))

    nblk = fb_ref.shape[-1]
    blk = lax.broadcasted_iota(jnp.int32, (1, nblk), 1)
    j_start = None
    for hh in range(2):
        qf = qt_ref[hh, 0:LANES, :].astype(F32)
        qn2 = jnp.max(jnp.sum(qf * qf, axis=0, keepdims=True), axis=1, keepdims=True)
        qk = jnp.sqrt(qn2 * kn_ref[hh:hh + 1, 0:1])
        thresh = PRUNE_BITS + PRUNE_SLACK * 2.0 * qk
        fb = fb_ref[0, 0, hh:hh + 1, :]
        f_ref = jnp.sum(jnp.where(blk == 2 * i - 1, fb, 0.0), axis=1, keepdims=True)
        dead = jnp.logical_and(fb - f_ref > thresh, blk < 2 * i)
        cnt = jnp.sum(jnp.where(dead, 1.0, 0.0)).astype(jnp.int32)
        j_start = cnt if j_start is None else jnp.minimum(j_start, cnt)
    t_start = j_start // 2

    def scores(j, s_ref, lo, hi):
        off = pl.multiple_of(j * tk, tk)
        kt = k_ref[0, pl.ds(off, tk), :]
        for hh in range(2):
            s_ref[hh, :, lo:hi] = jnp.dot(kt, qt_ref[hh, :, lo:hi],
                                          preferred_element_type=F32)

    def consume(j, s_ref, lo, hi, causal):
        off = pl.multiple_of(j * tk, tk)
        vt = v_ref[0, pl.ds(off, tk), :]
        for hh in range(2):
            s = s_ref[hh, :, lo:hi]
            if causal:
                kpos = lax.broadcasted_iota(jnp.int32, s.shape, 0)
                qpos = lax.broadcasted_iota(jnp.int32, s.shape, 1)
                s = jnp.where(kpos <= qpos, s, -jnp.inf)
            m = m_ref[hh, :, lo:hi]
            m_new = jnp.maximum(m, jnp.max(s, axis=0, keepdims=True))
            alpha = jnp.exp2(m - m_new)
            p = jnp.exp2(s - m_new).astype(BF16)
            own_v = jnp.logical_and(vlane >= hh * dh, vlane < (hh + 1) * dh)
            vh = jnp.where(own_v, vt, ones)
            pv = lax.dot_general(vh, p, (((0,), (0,)), ((), ())),
                                 preferred_element_type=F32)
            acc_ref[hh, :, lo:hi] = alpha * acc_ref[hh, :, lo:hi] + pv
            m_ref[hh, :, lo:hi] = m_new

    def body(t, _):
        j = 2 * t
        scores(j + 1, sb_ref, 0, tq)
        consume(j, sa_ref, 0, tq, False)
        scores(j + 2, sa_ref, 0, tq)
        consume(j + 1, sb_ref, 0, tq, False)
        return 0

    assert tq == 2 * tk
    scores(2 * t_start, sa_ref, 0, tq)
    lax.fori_loop(t_start, i, body, 0)
    j = 2 * i
    scores(j + 1, sb_ref, tk, tq)
    consume(j, sa_ref, 0, tk, True)
    consume(j, sa_ref, tk, tq, False)
    consume(j + 1, sb_ref, tk, tq, True)
    acc0 = acc_ref[0]
    acc1 = acc_ref[1]
    o_ref[0] = jnp.concatenate(
        [acc0[:dh] / acc0[dh:dh + 1], acc1[dh:] / acc1[0:1]], axis=0).astype(BF16)


def _fox(fq, fk, fv, f_end):
    bsz, seq, w = fv.shape
    dh = w // FOX_HEADS
    tq = min(ATT_TQ, seq)
    tk = tq // 2
    nblk = seq // tk
    assert 2 * dh == LANES and fq.shape[-1] == 2 * w
    assert f_end.shape == (bsz, FOX_HEADS // 2, 2, nblk)
    return pl.pallas_call(
        functools.partial(_fox_kernel, tk=tk, dh=dh),
        grid=(bsz, FOX_HEADS // 2, seq // tq),
        in_specs=[pl.BlockSpec((1, tq, 2 * LANES), lambda b, h, i: (b, i, h)),
                  pl.BlockSpec((1, seq, 2 * LANES), lambda b, h, i: (b, 0, h)),
                  pl.BlockSpec((1, seq, LANES), lambda b, h, i: (b, 0, h)),
                  pl.BlockSpec((1, 1, 2, nblk), lambda b, h, i: (b, h, 0, 0))],
        out_specs=pl.BlockSpec((1, LANES, tq), lambda b, h, i: (b, h, i)),
        out_shape=jax.ShapeDtypeStruct((bsz, w, seq), BF16),
        scratch_shapes=[pltpu.VMEM((2, 2 * LANES, tq), BF16),
                        pltpu.VMEM((2, tk, tq), F32), pltpu.VMEM((2, tk, tq), F32),
                        pltpu.VMEM((2, 1, tq), F32), pltpu.VMEM((2, LANES, tq), F32),
                        pltpu.VMEM((SUBLANES, LANES), F32)],
        compiler_params=_params("parallel", "parallel", "arbitrary"),
        name="fox_attention",
    )(fq, fk, fv, f_end)


def _merge_kernel(x_ref, ya_ref, yb_ref, yc_ref, wgate_ref, wbr_ref, wout_ref,
                  g_ref, b_ref, o_ref):
    x = x_ref[...]
    d = x.shape[-1]
    logits = jnp.dot(x.astype(BF16), wgate_ref[...], preferred_element_type=F32)
    merged = None
    for n in range(N_BRANCH):
        if n < 2:
            y = (ya_ref, yb_ref)[n][...]
            proj = jnp.dot(y, wbr_ref[n], preferred_element_type=F32)
        else:
            proj = lax.dot_general(yc_ref[0], wbr_ref[n], (((0,), (0,)), ((), ())),
                                   preferred_element_type=F32)
        term = jax.nn.sigmoid(logits[:, n * d:(n + 1) * d]) * proj
        merged = term if merged is None else merged + term
    mix = jnp.dot(merged.astype(BF16), wout_ref[...], preferred_element_type=F32)
    o_ref[...] = _layer_norm(DEEPNORM_ALPHA * x + mix, g_ref[...], b_ref[...])


def _merge(x2, ya, yb, yc_t, w_gate, w_branch, w_out, g, b):
    n, d = x2.shape
    bw = ya.shape[-1]
    seq = yc_t.shape[-1]
    tm = min(MERGE_TM, seq)
    per_seq = seq // tm
    tok = lambda w: pl.BlockSpec((tm, w), lambda i: (i, 0))
    return pl.pallas_call(
        _merge_kernel,
        grid=(n // tm,),
        in_specs=[tok(d), tok(bw), tok(bw),
                  pl.BlockSpec((1, bw, tm), lambda i: (i // per_seq, 0, i % per_seq)),
                  _const_spec((d, N_BRANCH * d)), _const_spec((N_BRANCH, bw, d)),
                  _const_spec((d, d)), _const_spec((1, d)), _const_spec((1, d))],
        out_specs=tok(d),
        out_shape=jax.ShapeDtypeStruct((n, d), F32),
        compiler_params=_params("parallel"),
        name="merge_out_ln",
    )(x2, ya, yb, yc_t, w_gate, w_branch, w_out, g, b)


def _ple_kernel(x_ref, p_ref, wp_ref, wg_ref, bg_ref, g_ref, b_ref, o_ref):
    x = x_ref[...]
    pe = jnp.dot(p_ref[...].astype(BF16), wp_ref[...], preferred_element_type=F32)
    gate = jax.nn.sigmoid(
        jnp.dot(x.astype(BF16), wg_ref[...], preferred_element_type=F32) + bg_ref[...])
    o_ref[...] = _layer_norm(DEEPNORM_ALPHA * x + gate * pe, g_ref[...], b_ref[...])


def _ple(x2, p2, w_proj, w_gate, b_gate, g, b):
    n, d = x2.shape
    pd = p2.shape[-1]
    tm = min(PLE_TM, n)
    tok = lambda w: pl.BlockSpec((tm, w), lambda i: (i, 0))
    return pl.pallas_call(
        _ple_kernel,
        grid=(n // tm,),
        in_specs=[tok(d), tok(pd), _const_spec((pd, d)), _const_spec((d, d)),
                  _const_spec((1, d)), _const_spec((1, d)), _const_spec((1, d))],
        out_specs=tok(d),
        out_shape=jax.ShapeDtypeStruct((n, d), F32),
        compiler_params=_params("parallel"),
        name="ple_ln",
    )(x2, p2, w_proj, w_gate, b_gate, g, b)


def _block_diag(w):
    nb, c, _ = w.shape
    eye = jnp.eye(nb, dtype=w.dtype)
    return (eye[:, None, :, None] * w[:, :, None, :]).reshape(nb * c, nb * c)


def _row(v):
    return v.reshape(1, -1).astype(F32)


def kernel(x, p, ffn1_w_up, ffn1_w_down, ln1_g, ln1_b, w_in, conv_w, conv_b, lru_wa, lru_ba, lru_wx, lru_bx, lru_lambda, gla_w_g2, gla_b_g, gla_norm_g, fox_b_f, w_branch, w_out, ln2_g, ln2_b, ffn2_w_up, ffn2_w_down, ln3_g, ln3_b, ple_w_proj, ple_w_gate, ple_b_gate, ln4_g, ln4_b):
    bsz, seq, d = x.shape
    depth = w_in.shape[0]
    dff = ffn1_w_down.shape[1]
    lru_w = conv_w.shape[-1]
    dk_all = gla_w_g2.shape[-1]
    dv_all = gla_norm_g.shape[-1]
    fox_w = w_branch.shape[2]
    n_low = gla_w_g2.shape[1]
    n_fh = fox_b_f.shape[-1]
    dh = fox_w // n_fh

    sizes = (lru_w, lru_w, dk_all, dk_all, dv_all, n_low, dv_all, fox_w, fox_w, fox_w,
             n_fh, N_BRANCH * d)
    offs = [0]
    for s in sizes:
        offs.append(offs[-1] + s)
    assert offs[-1] == w_in.shape[-1]
    tk = min(ATT_TQ, seq) // 2
    place = np.zeros((F_PIECES * LANES, (n_fh // 2) * LANES), np.float32)
    for h in range(n_fh):
        for r in range(F_PIECES):
            place[r * LANES + n_low + h, (h // 2) * LANES + (h % 2) * F_PIECES + r] = 1.0
    place = jnp.asarray(place, BF16)
    qaug = jnp.where(jnp.arange(LANES) < 2 * F_PIECES, -1.0, 0.0).astype(F32).reshape(1, LANES)

    mask =(jnp.arange(dv_all)[:, None] // (dv_all // GLA_HEADS)
            == jnp.arange(dk_all)[None, :] // (dk_all // GLA_HEADS)).astype(F32)

    x2 = x.reshape(bsz * seq, d)
    for i in range(depth):
        wi = w_in[i]
        col = lambda a: wi[:, offs[a]:offs[a + 1]]
        small_w = jnp.zeros((d, LANES), F32)
        small_w = small_w.at[:, :n_low].set(col(5)).at[:, n_low:n_low + n_fh].set(col(10))
        w_proj = jnp.concatenate(
            [col(0), col(1), col(2), col(3), col(4), col(6),
             col(7) * (dh ** -0.5 * LOG2E), col(8), col(9), small_w], axis=1).astype(BF16)
        bf_pad = jnp.zeros((1, LANES), F32).at[0, n_low:n_low + n_fh].set(fox_b_f[i])
        w_g2_pad = jnp.zeros((LANES, dk_all), F32).at[:n_low].set(gla_w_g2[i]).astype(BF16)
        w_lru_gates = jnp.concatenate(
            [_block_diag(lru_wa[i]), _block_diag(lru_wx[i])], axis=1).astype(BF16)

        x2 = _ffn(x2, ffn1_w_up[i][:, :dff].astype(BF16), ffn1_w_up[i][:, dff:].astype(BF16),
                  ffn1_w_down[i].astype(BF16), _row(ln1_g[i]), _row(ln1_b[i]))

        lru_in, gla_in, fq, fk, fv, small, fend = _proj(
            x2.reshape(bsz, seq, d), w_proj, bf_pad, place, qaug,
            (2 * lru_w, 2 * dk_all + 2 * dv_all, fox_w), (n_low, n_low + n_fh))
        ya = _lru(lru_in, conv_w[i].astype(F32), _row(conv_b[i]), w_lru_gates,
                  _row(lru_ba[i]), _row(lru_bx[i]), _row(lru_lambda[i]))
        yb = _gla(gla_in, small, w_g2_pad, _row(gla_b_g[i]), _row(gla_norm_g[i]), mask,
                  dk_all, dv_all)
        per_blk = tk // min(PROJ_TM, seq)
        f_end = fend[:, per_blk - 1::per_blk, 0, n_low:n_low + n_fh].transpose(0, 2, 1)
        yc_t = _fox(fq, fk, fv, f_end.reshape(bsz, n_fh // 2, 2, seq // tk))
        n = bsz * seq
        x2 = _merge(x2, ya.reshape(n, -1), yb.reshape(n, -1), yc_t,
                    col(11).astype(BF16), w_branch[i].astype(BF16), w_out[i].astype(BF16),
                    _row(ln2_g[i]), _row(ln2_b[i]))

        x2 = _ffn(x2, ffn2_w_up[i][:, :dff].astype(BF16), ffn2_w_up[i][:, dff:].astype(BF16),
                  ffn2_w_down[i].astype(BF16), _row(ln3_g[i]), _row(ln3_b[i]))

        x2 = _ple(x2, p[i].reshape(n, -1), ple_w_proj[i].astype(BF16),
                  ple_w_gate[i].astype(BF16), _row(ple_b_gate[i]),
                  _row(ln4_g[i]), _row(ln4_b[i]))
    return x2.reshape(bsz, seq, d)
```

```python
import functools

import jax
import jax.numpy as jnp
import numpy as np
from jax import lax
from jax.experimental import pallas as pl
from jax.experimental.pallas import tpu as pltpu

F32 = jnp.float32
BF16 = jnp.bfloat16

DEPTH = 2
CHUNK = 64
LRU_BLOCKS = 8
CONV_WIDTH = 4
LRU_C = 8.0
GLA_HEADS = 4
GLA_LOWRANK = 16
GLA_TAU = 16.0
FOX_HEADS = 8
N_BRANCH = 3
LN_EPS = 1e-5
RMS_EPS = 1e-6
DEEPNORM_ALPHA = (2 * DEPTH) ** 0.25

LOG2E = 1.4426950408889634
F_PIECES = 3
PRUNE_BITS = 150.0
PRUNE_SLACK = 1.05

LANES = 128
SUBLANES = 8
VMEM_LIMIT = 56 * 1024 * 1024

FFN_TM = 512
PROJ_TM = 512
ATT_TQ = 1024
MERGE_TM = 512
PLE_TM = 512


def _const_spec(shape):
    nd = len(shape)
    return pl.BlockSpec(shape, lambda *_: (0,) * nd, pipeline_mode=pl.Buffered(1))


def _layer_spec(arr, layer):
    nd = arr.ndim - 1
    return pl.BlockSpec((None,) + arr.shape[1:], lambda *_: (layer,) + (0,) * nd,
                        pipeline_mode=pl.Buffered(1))


def _params(*sem):
    return pltpu.CompilerParams(dimension_semantics=sem, vmem_limit_bytes=VMEM_LIMIT)


def _layer_norm(z, g, b):
    mu = jnp.mean(z, axis=-1, keepdims=True)
    d = z - mu
    var = jnp.mean(d * d, axis=-1, keepdims=True)
    return d * lax.rsqrt(var + LN_EPS) * g + b


def _softplus(z):
    return jnp.maximum(z, 0.0) + jnp.log1p(jnp.exp(-jnp.abs(z)))


def _log_sigmoid(z):
    return jnp.minimum(z, 0.0) - jnp.log1p(jnp.exp(-jnp.abs(z)))


def _silu(z):
    return z * jax.nn.sigmoid(z)


def _cumsum_rows(x, period):
    pos = lax.broadcasted_iota(jnp.int32, x.shape, 0) % period
    s = 1
    while s < period:
        x = x + jnp.where(pos >= s, pltpu.roll(x, s, axis=0), 0.0)
        s *= 2
    return x


def _ffn_kernel(x_ref, wg_ref, wu_ref, wd_ref, g_ref, b_ref, o_ref):
    x = x_ref[...]
    xb = x.astype(BF16)
    gate = jnp.dot(xb, wg_ref[...], preferred_element_type=F32)
    up = jnp.dot(xb, wu_ref[...], preferred_element_type=F32)
    act = (_silu(gate) * up).astype(BF16)
    y = jnp.dot(act, wd_ref[...], preferred_element_type=F32)
    o_ref[...] = _layer_norm(DEEPNORM_ALPHA * x + 0.5 * y, g_ref[...], b_ref[...])


def _ffn(x2, w_up, w_down, g, b, layer):
    n, d = x2.shape
    dff = w_down.shape[1]
    tm = min(FFN_TM, n)
    half = lambda h: pl.BlockSpec((None, d, dff), lambda i: (layer, 0, h),
                                  pipeline_mode=pl.Buffered(1))
    return pl.pallas_call(
        _ffn_kernel,
        grid=(n // tm,),
        in_specs=[pl.BlockSpec((tm, d), lambda i: (i, 0)),
                  half(0), half(1), _layer_spec(w_down, layer),
                  _layer_spec(g, layer), _layer_spec(b, layer)],
        out_specs=pl.BlockSpec((tm, d), lambda i: (i, 0)),
        out_shape=jax.ShapeDtypeStruct((n, d), F32),
        compiler_params=_params("parallel"),
        name="ffn_ln",
    )(x2, w_up, w_up, w_down, g, b)


def _proj_kernel(x_ref, w_ref, bf_ref, place_ref, qaug_ref,
                 cw_ref, cb_ref, wlg_ref, ba_ref, bx_ref, lam_ref,
                 wg2_ref, bg_ref, ng_ref, mask_ref,
                 out_ref, fend_ref,
                 carry_ref, halo_ref, h_ref, st_ref, *, widths, f_lanes):
    @pl.when(pl.program_id(1) == 0)
    def _():
        carry_ref[...] = jnp.zeros_like(carry_ref)
        halo_ref[...] = jnp.zeros_like(halo_ref)
        h_ref[...] = jnp.zeros_like(h_ref)
        st_ref[...] = jnp.zeros_like(st_ref)

    w_lru, w_gla, w_fox = widths
    cols = [0]
    for w in (w_lru, w_gla, w_fox, w_fox, w_fox, LANES):
        cols.append(cols[-1] + w)
    xb = x_ref[0].astype(BF16)
    rows = xb.shape[0]
    group = lambda n: jnp.dot(xb, w_ref[:, cols[n]:cols[n + 1]], preferred_element_type=F32)
    small = group(5)

    lane = lax.broadcasted_iota(jnp.int32, small.shape, 1)
    is_f = jnp.logical_and(lane >= f_lanes[0], lane < f_lanes[1])
    log_f = jnp.where(is_f, _log_sigmoid(small + bf_ref[...]), 0.0)
    f_cum = _cumsum_rows(log_f, rows) + carry_ref[0:1, :]
    carry_ref[0:1, :] = f_cum[rows - 1:, :]
    f2 = f_cum * LOG2E
    fend_ref[0, 0] = jnp.broadcast_to(f2[rows - 1:, :], (SUBLANES, LANES))
    hi = f2.astype(BF16)
    r1 = f2 - hi.astype(F32)
    mid = r1.astype(BF16)
    lo = (r1 - mid.astype(F32)).astype(BF16)
    bias = jnp.dot(jnp.concatenate([hi, mid, lo], axis=1), place_ref[...],
                   preferred_element_type=F32).astype(BF16)

    y_lru = group(0)
    ya = _lru_tile(y_lru[:, :w_lru // 2], y_lru[:, w_lru // 2:], cw_ref, cb_ref, wlg_ref,
                   ba_ref, bx_ref, lam_ref, halo_ref, h_ref)
    yb = _gla_tile(group(1), small, wg2_ref, bg_ref, ng_ref, mask_ref, st_ref,
                   mask_ref.shape[1], mask_ref.shape[0])

    yq = group(2).astype(BF16)
    yk = group(3).astype(BF16)
    qaug = jnp.broadcast_to(qaug_ref[...], (rows, LANES)).astype(BF16)
    fq, fk = [], []
    for pr in range(w_fox // LANES):
        src = slice(pr * LANES, (pr + 1) * LANES)
        fq += [yq[:, src], qaug]
        fk += [yk[:, src], bias[:, src]]
    fv = group(4).astype(BF16)
    out_ref[0] = jnp.concatenate([ya, yb] + fq + fk + [fv], axis=1)


def _proj(x, layered, place, qaug, mask, widths, f_lanes, layer):
    bsz, seq, d = x.shape
    tm = min(PROJ_TM, seq)
    w_lru, w_gla, w_fox = widths
    lru_w = w_lru // 2
    dv_all, dk_all = mask.shape
    tok = lambda w: pl.BlockSpec((1, tm, w), lambda b, i: (b, i, 0))
    return pl.pallas_call(
        functools.partial(_proj_kernel, widths=widths, f_lanes=f_lanes),
        grid=(bsz, seq // tm),
        in_specs=([tok(d), _layer_spec(layered[0], layer), _layer_spec(layered[1], layer),
                   _const_spec(place.shape), _const_spec(qaug.shape)]
                  + [_layer_spec(a, layer) for a in layered[2:]] + [_const_spec(mask.shape)]),
        out_specs=[tok(lru_w + dv_all + 5 * w_fox),
                   pl.BlockSpec((1, 1, SUBLANES, LANES), lambda b, i: (b, i, 0, 0))],
        out_shape=[jax.ShapeDtypeStruct((bsz, seq, lru_w + dv_all + 5 * w_fox), BF16),
                   jax.ShapeDtypeStruct((bsz, seq // tm, SUBLANES, LANES), F32)],
        scratch_shapes=[pltpu.VMEM((SUBLANES, LANES), F32),
                        pltpu.VMEM((SUBLANES, lru_w), F32),
                        pltpu.VMEM((SUBLANES, lru_w), F32),
                        pltpu.VMEM((dv_all, dk_all), F32)],
        compiler_params=_params("parallel", "arbitrary"),
        name="mixer_front",
    )(x, layered[0], layered[1], place, qaug, *layered[2:], mask)


def _lru_tile(ax, ay, cw_ref, cb_ref, wg_ref, ba_ref, bx_ref, lam_ref, halo_ref, h_ref):
    rows, width = ax.shape

    ext = jnp.concatenate([halo_ref[...], ax], axis=0)
    u = cb_ref[...] + cw_ref[CONV_WIDTH - 1:CONV_WIDTH, :] * ax
    for k in range(CONV_WIDTH - 1):
        shift = CONV_WIDTH - 1 - k
        u = u + cw_ref[k:k + 1, :] * ext[8 - shift:8 - shift + rows, :]
    halo_ref[...] = ax[rows - 8:, :]

    gates = jnp.dot(u.astype(BF16), wg_ref[...], preferred_element_type=F32)
    r = jax.nn.sigmoid(gates[:, :width] + ba_ref[...])
    i = jax.nn.sigmoid(gates[:, width:] + bx_ref[...])
    log_a = (-LRU_C) * r * _softplus(-lam_ref[...])
    a = jnp.exp(log_a)
    mult = jnp.sqrt(-jnp.tanh(log_a) * (1.0 + a * a))
    bterm = mult * (i * u)

    groups = rows // SUBLANES
    a3 = a.reshape(groups, SUBLANES, width)
    b3 = bterm.reshape(groups, SUBLANES, width)
    pos = lax.broadcasted_iota(jnp.int32, a3.shape, 1)
    s = 1
    while s < SUBLANES:
        keep = pos >= s
        a_sh = jnp.where(keep, pltpu.roll(a3, s, axis=1), 1.0)
        b_sh = jnp.where(keep, pltpu.roll(b3, s, axis=1), 0.0)
        b3 = b3 + a3 * b_sh
        a3 = a3 * a_sh
        s *= 2
    h_prev = h_ref[0:1, :]
    hs = []
    for gi in range(groups):
        hg = b3[gi] + a3[gi] * h_prev
        hs.append(hg)
        h_prev = hg[SUBLANES - 1:, :]
    h_ref[0:1, :] = h_prev
    h = jnp.concatenate(hs, axis=0)
    return (jax.nn.gelu(ay) * h).astype(BF16)


def _gla_tile(y, small, wg2_ref, bg_ref, ng_ref, mask_ref, st_ref, dk_all, dv_all):
    rows = y.shape[0]
    dv = dv_all // GLA_HEADS
    dk = dk_all // GLA_HEADS
    g = jnp.dot(small.astype(BF16), wg2_ref[...], preferred_element_type=F32) + bg_ref[...]
    log_alpha = _log_sigmoid(g) * (1.0 / GLA_TAU)
    g_cum = _cumsum_rows(log_alpha, CHUNK)

    chunks = []
    for c in range(rows // CHUNK):
        sl = slice(c * CHUNK, (c + 1) * CHUNK)
        q = (y[sl, :dk_all] * (dk ** -0.5)).astype(BF16)
        k = y[sl, dk_all:2 * dk_all]
        v = y[sl, 2 * dk_all:2 * dk_all + dv_all].astype(BF16)
        gate = y[sl, 2 * dk_all + dv_all:]
        gc = g_cum[sl, :]
        g_tot = gc[CHUNK - 1:, :]
        k_dec = (k * jnp.exp(g_tot - gc)).astype(BF16)
        delta = lax.dot_general(v, k_dec, (((0,), (0,)), ((), ())),
                                preferred_element_type=F32)
        st = st_ref[...] * jnp.exp(g_tot) + delta * mask_ref[...]
        st_ref[...] = st
        o = lax.dot_general(q, st.astype(BF16), (((1,), (1,)), ((), ())),
                            preferred_element_type=F32)
        outs = []
        for h in range(GLA_HEADS):
            oh = o[:, h * dv:(h + 1) * dv]
            ms = jnp.mean(oh * oh, axis=-1, keepdims=True)
            outs.append(oh * lax.rsqrt(ms + RMS_EPS))
        o = jnp.concatenate(outs, axis=-1) * ng_ref[...]
        chunks.append((o * _silu(gate)).astype(BF16))
    return jnp.concatenate(chunks, axis=0)


def _fox_kernel(q_ref, k_ref, v_ref, fb_ref, o_ref, qt_ref, sa_ref, sb_ref, m_ref,
                acc_ref, kn_ref, *, tk, dh):
    tq = q_ref.shape[1]
    seq = k_ref.shape[1]
    i = pl.program_id(2)
    q = q_ref[0]
    lane = lax.broadcasted_iota(jnp.int32, q.shape, 1)
    zero = jnp.zeros_like(q)
    for hh in range(2):
        own = jnp.logical_or(
            jnp.logical_and(lane >= hh * dh, lane < (hh + 1) * dh),
            jnp.logical_and(lane >= LANES + hh * F_PIECES, lane < LANES + (hh + 1) * F_PIECES))
        qt_ref[hh] = jnp.where(own, q, zero).T
    m_ref[...] = jnp.full(m_ref.shape, -jnp.inf, F32)
    acc_ref[...] = jnp.zeros(acc_ref.shape, F32)
    vlane = lax.broadcasted_iota(jnp.int32, (tk, LANES), 1)
    ones = jnp.ones((tk, LANES), BF16)

    @pl.when(i == 0)
    def _():
        def kn_body(c, mx):
            kt = k_ref[0, pl.ds(pl.multiple_of(c * tk, tk), tk), 0:LANES].astype(F32)
            sq = kt * kt
            out = []
            for hh in range(2):
                own_k = jnp.logical_and(vlane >= hh * dh, vlane < (hh + 1) * dh)
                n2 = jnp.sum(jnp.where(own_k, sq, 0.0), axis=1, keepdims=True)
                out.append(jnp.maximum(mx[hh], jnp.max(n2, axis=0, keepdims=True)))
            return tuple(out)
        mx = lax.fori_loop(0, seq // tk, kn_body,
                           (jnp.zeros((1, 1), F32), jnp.zeros((1, 1), F32)))
        for hh in range(2):
            kn_ref[hh:hh + 1, :] = jnp.broadcast_to(mx[hh], (1, LANES))

    nblk = fb_ref.shape[-1]
    blk = lax.broadcasted_iota(jnp.int32, (1, nblk), 1)
    j_start = None
    for hh in range(2):
        qf = qt_ref[hh, 0:LANES, :].astype(F32)
        qn2 = jnp.max(jnp.sum(qf * qf, axis=0, keepdims=True), axis=1, keepdims=True)
        qk = jnp.sqrt(qn2 * kn_ref[hh:hh + 1, 0:1])
        thresh = PRUNE_BITS + PRUNE_SLACK * 2.0 * qk
        fb = fb_ref[0, 0, hh:hh + 1, :]
        f_ref = jnp.sum(jnp.where(blk == 2 * i - 1, fb, 0.0), axis=1, keepdims=True)
        dead = jnp.logical_and(fb - f_ref > thresh, blk < 2 * i)
        cnt = jnp.sum(jnp.where(dead, 1.0, 0.0)).astype(jnp.int32)
        j_start = cnt if j_start is None else jnp.minimum(j_start, cnt)
    t_start = j_start // 2

    def scores(j, s_ref, lo, hi):
        off = pl.multiple_of(j * tk, tk)
        kt = k_ref[0, pl.ds(off, tk), :]
        for hh in range(2):
            s_ref[hh, :, lo:hi] = jnp.dot(kt, qt_ref[hh, :, lo:hi],
                                          preferred_element_type=F32)

    def consume(j, s_ref, lo, hi, causal):
        off = pl.multiple_of(j * tk, tk)
        vt = v_ref[0, pl.ds(off, tk), :]
        for hh in range(2):
            s = s_ref[hh, :, lo:hi]
            if causal:
                kpos = lax.broadcasted_iota(jnp.int32, s.shape, 0)
                qpos = lax.broadcasted_iota(jnp.int32, s.shape, 1)
                s = jnp.where(kpos <= qpos, s, -jnp.inf)
            m = m_ref[hh, :, lo:hi]
            m_new = jnp.maximum(m, jnp.max(s, axis=0, keepdims=True))
            alpha = jnp.exp2(m - m_new)
            p = jnp.exp2(s - m_new).astype(BF16)
            own_v = jnp.logical_and(vlane >= hh * dh, vlane < (hh + 1) * dh)
            vh = jnp.where(own_v, vt, ones)
            pv = lax.dot_general(vh, p, (((0,), (0,)), ((), ())),
                                 preferred_element_type=F32)
            acc_ref[hh, :, lo:hi] = alpha * acc_ref[hh, :, lo:hi] + pv
            m_ref[hh, :, lo:hi] = m_new

    def body(t, _):
        j = 2 * t
        scores(j + 1, sb_ref, 0, tq)
        consume(j, sa_ref, 0, tq, False)
        scores(j + 2, sa_ref, 0, tq)
        consume(j + 1, sb_ref, 0, tq, False)
        return 0

    assert tq == 2 * tk
    scores(2 * t_start, sa_ref, 0, tq)
    lax.fori_loop(t_start, i, body, 0)
    j = 2 * i
    scores(j + 1, sb_ref, tk, tq)
    consume(j, sa_ref, 0, tk, True)
    consume(j, sa_ref, tk, tq, False)
    consume(j + 1, sb_ref, tk, tq, True)
    acc0 = acc_ref[0]
    acc1 = acc_ref[1]
    o_ref[0] = jnp.concatenate(
        [acc0[:dh] / acc0[dh:dh + 1], acc1[dh:] / acc1[0:1]], axis=0).astype(BF16)


def _fox(mix, cols, w, f_end):
    bsz, seq, _ = mix.shape
    dh = w // FOX_HEADS
    tq = min(ATT_TQ, seq)
    tk = tq // 2
    nblk = seq // tk
    assert 2 * dh == LANES
    q0, k0, v0 = cols[0] // (2 * LANES), cols[1] // (2 * LANES), cols[2] // LANES
    assert f_end.shape == (bsz, FOX_HEADS // 2, 2, nblk)
    return pl.pallas_call(
        functools.partial(_fox_kernel, tk=tk, dh=dh),
        grid=(bsz, FOX_HEADS // 2, seq // tq),
        in_specs=[pl.BlockSpec((1, tq, 2 * LANES), lambda b, h, i: (b, i, q0 + h)),
                  pl.BlockSpec((1, seq, 2 * LANES), lambda b, h, i: (b, 0, k0 + h)),
                  pl.BlockSpec((1, seq, LANES), lambda b, h, i: (b, 0, v0 + h)),
                  pl.BlockSpec((1, 1, 2, nblk), lambda b, h, i: (b, h, 0, 0))],
        out_specs=pl.BlockSpec((1, LANES, tq), lambda b, h, i: (b, h, i)),
        out_shape=jax.ShapeDtypeStruct((bsz, w, seq), BF16),
        scratch_shapes=[pltpu.VMEM((2, 2 * LANES, tq), BF16),
                        pltpu.VMEM((2, tk, tq), F32), pltpu.VMEM((2, tk, tq), F32),
                        pltpu.VMEM((2, 1, tq), F32), pltpu.VMEM((2, LANES, tq), F32),
                        pltpu.VMEM((SUBLANES, LANES), F32)],
        compiler_params=_params("parallel", "parallel", "arbitrary"),
        name="fox_attention",
    )(mix, mix, mix, f_end)


def _merge_kernel(x_ref, ya_ref, yb_ref, yc_ref, wgate_ref, wbr_ref, wout_ref,
                  g_ref, b_ref, o_ref):
    x = x_ref[...]
    d = x.shape[-1]
    logits = jnp.dot(x.astype(BF16), wgate_ref[...], preferred_element_type=F32)
    merged = None
    for n in range(N_BRANCH):
        if n < 2:
            y = (ya_ref, yb_ref)[n][...]
            proj = jnp.dot(y, wbr_ref[n], preferred_element_type=F32)
        else:
            proj = lax.dot_general(yc_ref[0], wbr_ref[n], (((0,), (0,)), ((), ())),
                                   preferred_element_type=F32)
        term = jax.nn.sigmoid(logits[:, n * d:(n + 1) * d]) * proj
        merged = term if merged is None else merged + term
    mix = jnp.dot(merged.astype(BF16), wout_ref[...], preferred_element_type=F32)
    o_ref[...] = _layer_norm(DEEPNORM_ALPHA * x + mix, g_ref[...], b_ref[...])


def _merge(x2, mix2, yc_t, w_gate, w_branch, w_out, g, b, layer):
    n, d = x2.shape
    bw = yc_t.shape[1]
    seq = yc_t.shape[-1]
    tm = min(MERGE_TM, seq)
    per_seq = seq // tm
    tok = lambda w: pl.BlockSpec((tm, w), lambda i: (i, 0))
    return pl.pallas_call(
        _merge_kernel,
        grid=(n // tm,),
        in_specs=[tok(d), pl.BlockSpec((tm, bw), lambda i: (i, 0)),
                  pl.BlockSpec((tm, bw), lambda i: (i, 1)),
                  pl.BlockSpec((1, bw, tm), lambda i: (i // per_seq, 0, i % per_seq)),
                  _layer_spec(w_gate, layer), _layer_spec(w_branch, layer),
                  _layer_spec(w_out, layer), _layer_spec(g, layer), _layer_spec(b, layer)],
        out_specs=tok(d),
        out_shape=jax.ShapeDtypeStruct((n, d), F32),
        compiler_params=_params("parallel"),
        name="merge_out_ln",
    )(x2, mix2, mix2, yc_t, w_gate, w_branch, w_out, g, b)


def _ple_kernel(x_ref, p_ref, wp_ref, wg_ref, bg_ref, g_ref, b_ref, o_ref):
    x = x_ref[...]
    pe = jnp.dot(p_ref[...].astype(BF16), wp_ref[...], preferred_element_type=F32)
    gate = jax.nn.sigmoid(
        jnp.dot(x.astype(BF16), wg_ref[...], preferred_element_type=F32) + bg_ref[...])
    o_ref[...] = _layer_norm(DEEPNORM_ALPHA * x + gate * pe, g_ref[...], b_ref[...])


def _ple(x2, p3, w_proj, w_gate, b_gate, g, b, layer):
    n, d = x2.shape
    pd = p3.shape[-1]
    tm = min(PLE_TM, n)
    tok = lambda w: pl.BlockSpec((tm, w), lambda i: (i, 0))
    return pl.pallas_call(
        _ple_kernel,
        grid=(n // tm,),
        in_specs=[tok(d), pl.BlockSpec((None, tm, pd), lambda i: (layer, i, 0)),
                  _layer_spec(w_proj, layer), _layer_spec(w_gate, layer),
                  _layer_spec(b_gate, layer), _layer_spec(g, layer), _layer_spec(b, layer)],
        out_specs=tok(d),
        out_shape=jax.ShapeDtypeStruct((n, d), F32),
        compiler_params=_params("parallel"),
        name="ple_ln",
    )(x2, p3, w_proj, w_gate, b_gate, g, b)


def _block_diag(w):
    dp, nb, c, _ = w.shape
    eye = jnp.eye(nb, dtype=w.dtype)
    return (eye[None, :, None, :, None] * w[:, :, :, None, :]).reshape(dp, nb * c, nb * c)


def _rows(v):
    return v.reshape(v.shape[0], 1, -1).astype(F32)


def kernel(x, p, ffn1_w_up, ffn1_w_down, ln1_g, ln1_b, w_in, conv_w, conv_b, lru_wa, lru_ba, lru_wx, lru_bx, lru_lambda, gla_w_g2, gla_b_g, gla_norm_g, fox_b_f, w_branch, w_out, ln2_g, ln2_b, ffn2_w_up, ffn2_w_down, ln3_g, ln3_b, ple_w_proj, ple_w_gate, ple_b_gate, ln4_g, ln4_b):
    bsz, seq, d = x.shape
    depth = w_in.shape[0]
    lru_w = conv_w.shape[-1]
    dk_all = gla_w_g2.shape[-1]
    dv_all = gla_norm_g.shape[-1]
    fox_w = w_branch.shape[2]
    n_low = gla_w_g2.shape[1]
    n_fh = fox_b_f.shape[-1]
    dh = fox_w // n_fh
    n = bsz * seq

    sizes = (lru_w, lru_w, dk_all, dk_all, dv_all, n_low, dv_all, fox_w, fox_w, fox_w,
             n_fh, N_BRANCH * d)
    offs = [0]
    for s in sizes:
        offs.append(offs[-1] + s)
    assert offs[-1] == w_in.shape[-1]
    col = lambda a: w_in[:, :, offs[a]:offs[a + 1]]
    tk = min(ATT_TQ, seq) // 2

    place = np.zeros((F_PIECES * LANES, (n_fh // 2) * LANES), np.float32)
    for h in range(n_fh):
        for r in range(F_PIECES):
            place[r * LANES + n_low + h, (h // 2) * LANES + (h % 2) * F_PIECES + r] = 1.0
    place = jnp.asarray(place, BF16)
    qaug = jnp.where(jnp.arange(LANES) < 2 * F_PIECES, -1.0, 0.0).astype(F32).reshape(1, LANES)
    mask = (jnp.arange(dv_all)[:, None] // (dv_all // GLA_HEADS)
            == jnp.arange(dk_all)[None, :] // (dk_all // GLA_HEADS)).astype(F32)

    small_w = jnp.zeros((depth, d, LANES), F32)
    small_w = small_w.at[:, :, :n_low].set(col(5)).at[:, :, n_low:n_low + n_fh].set(col(10))
    w_proj = jnp.concatenate(
        [col(0), col(1), col(2), col(3), col(4), col(6),
         col(7) * (dh ** -0.5 * LOG2E), col(8), col(9), small_w], axis=2).astype(BF16)
    bf_pad = jnp.zeros((depth, 1, LANES), F32).at[:, 0, n_low:n_low + n_fh].set(fox_b_f)
    w_g2_pad = jnp.zeros((depth, LANES, dk_all), F32).at[:, :n_low].set(gla_w_g2).astype(BF16)
    w_lru_gates = jnp.concatenate(
        [_block_diag(lru_wa), _block_diag(lru_wx)], axis=2).astype(BF16)
    front = (w_proj, bf_pad, conv_w.astype(F32), _rows(conv_b), w_lru_gates, _rows(lru_ba),
             _rows(lru_bx), _rows(lru_lambda), w_g2_pad, _rows(gla_b_g), _rows(gla_norm_g))
    w_mgate = col(11).astype(BF16)
    w_br = w_branch.astype(BF16)
    w_o = w_out.astype(BF16)
    ffn1 = (ffn1_w_up.astype(BF16), ffn1_w_down.astype(BF16), _rows(ln1_g), _rows(ln1_b))
    ffn2 = (ffn2_w_up.astype(BF16), ffn2_w_down.astype(BF16), _rows(ln3_g), _rows(ln3_b))
    ple = (p.reshape(depth, n, -1), ple_w_proj.astype(BF16), ple_w_gate.astype(BF16),
           _rows(ple_b_gate), _rows(ln4_g), _rows(ln4_b))
    ln2 = (_rows(ln2_g), _rows(ln2_b))

    x2 = x.reshape(n, d)
    for i in range(depth):
        x2 = _ffn(x2, *ffn1, i)
        mix, fend = _proj(
            x2.reshape(bsz, seq, d), front, place, qaug, mask,
            (2 * lru_w, 2 * dk_all + 2 * dv_all, fox_w), (n_low, n_low + n_fh), i)
        per_blk = tk // min(PROJ_TM, seq)
        f_end = fend[:, per_blk - 1::per_blk, 0, n_low:n_low + n_fh].transpose(0, 2, 1)
        fox_cols = (lru_w + dv_all, lru_w + dv_all + 2 * fox_w, lru_w + dv_all + 4 * fox_w)
        yc_t = _fox(mix, fox_cols, fox_w, f_end.reshape(bsz, n_fh // 2, 2, seq // tk))
        x2 = _merge(x2, mix.reshape(n, -1), yc_t, w_mgate, w_br, w_o, *ln2, i)
        x2 = _ffn(x2, *ffn2, i)
        x2 = _ple(x2, *ple, i)
    return x2.reshape(bsz, seq, d)
```
